```python
import jax, jax.numpy as jnp
from jax import lax
import numpy as np

D_MODEL = 2048
BATCH = 2
SEQ = 8192
DEPTH = 1

GLA_HEADS = 4
GLA_KEY_DIM = D_MODEL // 2
GLA_VALUE_DIM = D_MODEL
GLA_HEAD_K = GLA_KEY_DIM // GLA_HEADS
GLA_HEAD_V = GLA_VALUE_DIM // GLA_HEADS
GK_RANK = 16
GATE_LOGIT_NORMALIZER = 16.0
CHUNK = 64
CONV_WIDTH = D_MODEL
CONV_K = 3
FFN_HIDDEN = ((8 * D_MODEL + 3 * 256 - 1) // (3 * 256)) * 256
EPS = 1e-6

IN_SPLITS = [
    GLA_KEY_DIM,
    GLA_KEY_DIM,
    GLA_VALUE_DIM,
    GLA_VALUE_DIM,
    GK_RANK,
    CONV_WIDTH,
    CONV_WIDTH,
    CONV_WIDTH,
    D_MODEL,
    D_MODEL,
]
IN_COLS = int(sum(IN_SPLITS))
SPLIT_IDX = [int(i) for i in np.cumsum(IN_SPLITS)[:-1]]

kernel_name = "hybrid_gla_shortconv_gated_merge"


def rmsnorm(x, w):
    xf = x.astype(jnp.float32)
    y = xf * lax.rsqrt(jnp.mean(xf * xf, axis=-1, keepdims=True) + EPS)
    return (y * w.astype(jnp.float32)).astype(x.dtype)


def gla_chunked(q, k, v, log_g):
    b_, s, h, dk = q.shape
    dv = v.shape[-1]
    n = s // CHUNK
    f32 = jnp.float32

    def chunks(t):
        return t.astype(f32).reshape(b_, n, CHUNK, h, t.shape[-1]).transpose(1, 0, 3, 2, 4)

    qc = chunks(q) * (dk ** -0.5)
    kc = chunks(k)
    vc = chunks(v)
    gc = chunks(log_g)
    bcum = jnp.cumsum(gc, axis=3)
    b_last = bcum[..., -1:, :]
    q_dec = qc * jnp.exp(bcum)
    k_dec = kc * jnp.exp(-bcum)
    k_to_end = kc * jnp.exp(b_last - bcum)

    causal = jnp.tril(jnp.ones((CHUNK, CHUNK), dtype=bool))
    scores = jnp.einsum('nbhik,nbhjk->nbhij', q_dec, k_dec)
    scores = jnp.where(causal, scores, 0.0)
    o_intra = jnp.einsum('nbhij,nbhjv->nbhiv', scores, vc)

    def step(state, inp):
        q_d, k_e, v_n, decay = inp
        o_inter = jnp.einsum('bhik,bhkv->bhiv', q_d, state)
        state = state * decay[:, :, 0, :, None] + jnp.einsum('bhjk,bhjv->bhkv', k_e, v_n)
        return state, o_inter

    state0 = jnp.zeros((b_, h, dk, dv), f32)
    _, o_inter = lax.scan(step, state0, (q_dec, k_to_end, vc, jnp.exp(b_last)))
    o = o_intra + o_inter
    return o.transpose(1, 0, 3, 2, 4).reshape(b_, s, h, dv)


def causal_depthwise_conv(u, w):
    s = u.shape[1]
    up = jnp.pad(u, ((0, 0), (CONV_K - 1, 0), (0, 0)))
    return sum(up[:, j:j + s, :] * w[j] for j in range(CONV_K))


def hybrid_mixer(h, w_in, w_gk_up, b_gk_up, gla_norm_w, conv_w, w_out):
    b_, s, _ = h.shape
    proj = jnp.einsum('bsd,de->bse', h, w_in)
    (q, k, v, g_out, gk_low, gate_b, gate_c, xc, merge_a, merge_b) = jnp.split(proj, SPLIT_IDX, axis=-1)

    gk = jnp.einsum('bsr,rk->bsk', gk_low, w_gk_up) + b_gk_up
    log_g = jax.nn.log_sigmoid(gk.astype(jnp.float32)) / GATE_LOGIT_NORMALIZER
    o = gla_chunked(q.reshape(b_, s, GLA_HEADS, GLA_HEAD_K),
                    k.reshape(b_, s, GLA_HEADS, GLA_HEAD_K),
                    v.reshape(b_, s, GLA_HEADS, GLA_HEAD_V),
                    log_g.reshape(b_, s, GLA_HEADS, GLA_HEAD_K))
    o = rmsnorm(o, gla_norm_w).astype(h.dtype)
    y_a = (o * jax.nn.silu(g_out.reshape(b_, s, GLA_HEADS, GLA_HEAD_V))).reshape(b_, s, GLA_VALUE_DIM)

    y_b = gate_b * causal_depthwise_conv(gate_c * xc, conv_w)

    merged = jax.nn.sigmoid(merge_a) * y_a + jax.nn.sigmoid(merge_b) * y_b
    return jnp.einsum('bse,ed->bsd', merged, w_out)


def swiglu(h, w_gate_up, w_down):
    gu = jnp.einsum('bsd,df->bsf', h, w_gate_up)
    gate, up = jnp.split(gu, [FFN_HIDDEN], axis=-1)
    return jnp.einsum('bsf,fd->bsd', jax.nn.silu(gate) * up, w_down)


def setup_inputs(seed: int = 0) -> dict:
    key = jax.random.key(seed)
    ks = jax.random.split(key, 12)
    f32 = jnp.float32
    nrm = lambda k, shape, scale: jax.random.normal(k, shape, f32) * scale
    return {
        "x": nrm(ks[0], (BATCH, SEQ, D_MODEL), 1.0),
        "mix_norm_w": 1.0 + nrm(ks[1], (DEPTH, D_MODEL), 0.02),
        "w_in": nrm(ks[2], (DEPTH, D_MODEL, IN_COLS), D_MODEL ** -0.5),
        "w_gk_up": nrm(ks[3], (DEPTH, GK_RANK, GLA_KEY_DIM), GK_RANK ** -0.5),
        "b_gk_up": nrm(ks[4], (DEPTH, GLA_KEY_DIM), 0.1),
        "gla_norm_w": 1.0 + nrm(ks[5], (DEPTH, GLA_HEAD_V), 0.02),
        "conv_w": nrm(ks[6], (DEPTH, CONV_K, CONV_WIDTH), CONV_K ** -0.5),
        "w_out": nrm(ks[7], (DEPTH, D_MODEL, D_MODEL), D_MODEL ** -0.5),
        "ffn_norm_w": 1.0 + nrm(ks[8], (DEPTH, D_MODEL), 0.02),
        "w_gate_up": nrm(ks[9], (DEPTH, D_MODEL, 2 * FFN_HIDDEN), D_MODEL ** -0.5),
        "w_down": nrm(ks[10], (DEPTH, FFN_HIDDEN, D_MODEL), FFN_HIDDEN ** -0.5),
        "final_norm_w": 1.0 + nrm(ks[11], (D_MODEL,), 0.02),
    }


def reference(x, mix_norm_w, w_in, w_gk_up, b_gk_up, gla_norm_w, conv_w, w_out,
              ffn_norm_w, w_gate_up, w_down, final_norm_w):
    h = x
    for l in range(DEPTH):
        h = h + hybrid_mixer(rmsnorm(h, mix_norm_w[l]), w_in[l], w_gk_up[l], b_gk_up[l],
                             gla_norm_w[l], conv_w[l], w_out[l])
        h = h + swiglu(rmsnorm(h, ffn_norm_w[l]), w_gate_up[l], w_down[l])
    return rmsnorm(h, final_norm_w)
```

```python
import functools

import jax
import jax.numpy as jnp
from jax import lax
from jax.experimental import pallas as pl
from jax.experimental.pallas import tpu as pltpu

EPS = 1e-6
GLA_HEADS = 4
GK_RANK = 16
GATE_LOGIT_NORMALIZER = 16.0
CHUNK = 64
CONV_K = 3

LANES = 128
SUBLANES = 8
VMEM_LIMIT = 56 * 1024 * 1024

F32 = jnp.float32
BF16 = jnp.bfloat16


def _params(semantics):
    return pltpu.CompilerParams(dimension_semantics=semantics,
                                vmem_limit_bytes=VMEM_LIMIT)


def _sigmoid(x):
    return 1.0 / (1.0 + jnp.exp(-x))


def _dot(a, b):
    return jnp.dot(a, b, preferred_element_type=F32)


def _rmsnorm_kernel(x_ref, w_ref, o_ref):
    x = x_ref[...]
    y = x * lax.rsqrt(jnp.mean(x * x, axis=-1, keepdims=True) + EPS)
    o_ref[...] = (y * w_ref[...]).astype(o_ref.dtype)


def _rmsnorm(x, w, tm=512):
    m, d = x.shape
    return pl.pallas_call(
        _rmsnorm_kernel,
        out_shape=jax.ShapeDtypeStruct((m, d), BF16),
        grid=(m // tm,),
        in_specs=[pl.BlockSpec((tm, d), lambda i: (i, 0)),
                  pl.BlockSpec((1, d), lambda i: (0, 0))],
        out_specs=pl.BlockSpec((tm, d), lambda i: (i, 0)),
        compiler_params=_params(("arbitrary",)),
        name="rmsnorm",
    )(x, w.reshape(1, d))


def _plain_kernel(h_ref, w_ref, o_ref):
    o_ref[...] = _dot(h_ref[...], w_ref[...]).astype(o_ref.dtype)


def _in_plain(hn, w, tm=1024, tn=512):
    m, d = hn.shape
    n = w.shape[1]
    return pl.pallas_call(
        _plain_kernel,
        out_shape=jax.ShapeDtypeStruct((m, n), BF16),
        grid=(m // tm, n // tn),
        in_specs=[pl.BlockSpec((tm, d), lambda i, j: (i, 0)),
                  pl.BlockSpec((d, tn), lambda i, j: (0, j))],
        out_specs=pl.BlockSpec((tm, tn), lambda i, j: (i, j)),
        compiler_params=_params(("arbitrary", "arbitrary")),
        name="in_plain",
    )(hn, w)


def _gate_a_kernel(h_ref, wg_ref, wm_ref, o_ref):
    h = h_ref[...]
    g = _dot(h, wg_ref[...])
    m = _dot(h, wm_ref[...])
    o_ref[...] = (g * _sigmoid(g) * _sigmoid(m)).astype(o_ref.dtype)


def _in_gate_a(hn, wg, wm, tm=1024, tn=512):
    m, d = hn.shape
    n = wg.shape[1]
    wspec = pl.BlockSpec((d, tn), lambda i, j: (0, j))
    return pl.pallas_call(
        _gate_a_kernel,
        out_shape=jax.ShapeDtypeStruct((m, n), BF16),
        grid=(m // tm, n // tn),
        in_specs=[pl.BlockSpec((tm, d), lambda i, j: (i, 0)), wspec, wspec],
        out_specs=pl.BlockSpec((tm, tn), lambda i, j: (i, j)),
        compiler_params=_params(("arbitrary", "arbitrary")),
        name="in_gate_a",
    )(hn, wg, wm)


def _conv_kernel(h_ref, wb_ref, wc_ref, wx_ref, wm_ref, cw_ref, o_ref,
                 work_ref, halo_ref, *, tm, tiles_per_seq):
    i = pl.program_id(0)
    j = pl.program_id(1)
    h = h_ref[...]
    u = _dot(h, wc_ref[...]) * _dot(h, wx_ref[...])

    @pl.when(i % tiles_per_seq == 0)
    def _():
        work_ref[0:SUBLANES, :] = jnp.zeros((SUBLANES, u.shape[1]), F32)

    @pl.when(i % tiles_per_seq != 0)
    def _():
        work_ref[0:SUBLANES, :] = halo_ref[j]

    work_ref[SUBLANES:SUBLANES + tm, :] = u
    halo_ref[j] = u[tm - SUBLANES:, :]

    u1 = work_ref[SUBLANES - 1:SUBLANES - 1 + tm, :]
    u2 = work_ref[SUBLANES - 2:SUBLANES - 2 + tm, :]
    cw = cw_ref[...]
    conv = cw[0:1, :] * u2 + cw[1:2, :] * u1 + cw[2:3, :] * u
    y_b = _dot(h, wb_ref[...]) * conv
    o_ref[...] = (_sigmoid(_dot(h, wm_ref[...])) * y_b).astype(o_ref.dtype)


def _in_conv(hn, wb, wc, wx, wm, conv_w, seq, tm=1024, tn=512):
    m, d = hn.shape
    n = wb.shape[1]
    wspec = pl.BlockSpec((d, tn), lambda i, j: (0, j))
    kern = functools.partial(_conv_kernel, tm=tm, tiles_per_seq=seq // tm)
    return pl.pallas_call(
        kern,
        out_shape=jax.ShapeDtypeStruct((m, n), BF16),
        grid=(m // tm, n // tn),
        in_specs=[pl.BlockSpec((tm, d), lambda i, j: (i, 0)),
                  wspec, wspec, wspec, wspec,
                  pl.BlockSpec((SUBLANES, tn), lambda i, j: (0, j))],
        out_specs=pl.BlockSpec((tm, tn), lambda i, j: (i, j)),
        scratch_shapes=[pltpu.VMEM((tm + SUBLANES, tn), F32),
                        pltpu.VMEM((n // tn, SUBLANES, tn), F32)],
        compiler_params=_params(("arbitrary", "arbitrary")),
        name="in_conv",
    )(hn, wb, wc, wx, wm, conv_w)


def _decay_kernel(h_ref, wgk_ref, wup_ref, b_ref, o_ref):
    low = _dot(h_ref[...], wgk_ref[...]).astype(BF16)
    gk = _dot(low, wup_ref[...]) + b_ref[...]
    log_sig = jnp.minimum(gk, 0.0) - jnp.log1p(jnp.exp(-jnp.abs(gk)))
    o_ref[...] = log_sig / GATE_LOGIT_NORMALIZER


def _in_decay(hn, wgk, wup, b, tm=512):
    m, d = hn.shape
    r = wgk.shape[1]
    n = wup.shape[1]
    return pl.pallas_call(
        _decay_kernel,
        out_shape=jax.ShapeDtypeStruct((m, n), F32),
        grid=(m // tm,),
        in_specs=[pl.BlockSpec((tm, d), lambda i: (i, 0)),
                  pl.BlockSpec((d, r), lambda i: (0, 0)),
                  pl.BlockSpec((r, n), lambda i: (0, 0)),
                  pl.BlockSpec((1, n), lambda i: (0, 0))],
        out_specs=pl.BlockSpec((tm, n), lambda i: (i, 0)),
        compiler_params=_params(("arbitrary",)),
        name="in_decay",
    )(hn, wgk, wup, b)


def _gla_kernel(q_ref, k_ref, v_ref, g_ref, ga_ref, yb_ref, nw_ref, o_ref,
                state_ref, *, rows, dk, dv):
    @pl.when(pl.program_id(1) == 0)
    def _():
        state_ref[...] = jnp.zeros(state_ref.shape, F32)

    row = lax.broadcasted_iota(jnp.int32, (CHUNK, CHUNK), 0)
    col = lax.broadcasted_iota(jnp.int32, (CHUNK, CHUNK), 1)
    causal = row >= col
    tri = causal.astype(F32)
    ones = jnp.ones((CHUNK, LANES), F32)
    scale = dk ** -0.5
    nw = nw_ref[...]

    def chunk_body(c, carry):
        r0 = pl.multiple_of(c * CHUNK, CHUNK)
        rs = pl.ds(r0, CHUNK)
        g = g_ref[rs, :]
        bcum = jnp.dot(tri, g, precision=lax.Precision.HIGHEST,
                       preferred_element_type=F32)
        b_last = bcum[CHUNK - 1:CHUNK, :]
        q = q_ref[rs, :].astype(F32)
        k = k_ref[rs, :].astype(F32)
        q_dec = (q * scale * jnp.exp(bcum)).astype(BF16)
        k_dec = (k * jnp.exp(-bcum)).astype(BF16)
        k_end = (k * jnp.exp(b_last - bcum)).astype(BF16)
        b_last_col = lax.dot_general(g, ones, (((0,), (0,)), ((), ())),
                                     precision=lax.Precision.HIGHEST,
                                     preferred_element_type=F32)
        decay_col = jnp.exp(b_last_col)
        for h in range(GLA_HEADS):
            ks = slice(h * dk, (h + 1) * dk)
            vs = slice(h * dv, (h + 1) * dv)
            v = v_ref[rs, vs]
            state = state_ref[h]
            scores = lax.dot_general(q_dec[:, ks], k_dec[:, ks],
                                     (((1,), (1,)), ((), ())),
                                     preferred_element_type=F32)
            scores = jnp.where(causal, scores, 0.0).astype(BF16)
            o = _dot(scores, v) + _dot(q_dec[:, ks], state.astype(BF16))
            kv = lax.dot_general(k_end[:, ks], v, (((0,), (0,)), ((), ())),
                                 preferred_element_type=F32)
            decay = jnp.concatenate([decay_col[ks, :]] * (dv // LANES), axis=1)
            state_ref[h] = state * decay + kv
            y = o * lax.rsqrt(jnp.mean(o * o, axis=-1, keepdims=True) + EPS) * nw
            merged = y * ga_ref[rs, vs].astype(F32) + yb_ref[rs, vs].astype(F32)
            o_ref[rs, vs] = merged.astype(o_ref.dtype)
        return carry

    lax.fori_loop(0, rows // CHUNK, chunk_body, 0)


def _gla_merge(qkv, log_g, gate_a, y_b, norm_w, batch, seq, rows=256):
    m = qkv.shape[0]
    kd = log_g.shape[1]
    vd = gate_a.shape[1]
    dk = kd // GLA_HEADS
    dv = vd // GLA_HEADS
    nblk = seq // rows
    assert qkv.shape[1] == 2 * kd + vd and 2 * kd == vd
    row_map = lambda b, s: (b * nblk + s, 0)
    kern = functools.partial(_gla_kernel, rows=rows, dk=dk, dv=dv)
    return pl.pallas_call(
        kern,
        out_shape=jax.ShapeDtypeStruct((m, vd), BF16),
        grid=(batch, nblk),
        in_specs=[pl.BlockSpec((rows, kd), lambda b, s: (b * nblk + s, 0)),
                  pl.BlockSpec((rows, kd), lambda b, s: (b * nblk + s, 1)),
                  pl.BlockSpec((rows, vd), lambda b, s: (b * nblk + s, 1)),
                  pl.BlockSpec((rows, kd), row_map),
                  pl.BlockSpec((rows, vd), row_map),
                  pl.BlockSpec((rows, vd), row_map),
                  pl.BlockSpec((1, dv), lambda b, s: (0, 0))],
        out_specs=pl.BlockSpec((rows, vd), row_map),
        scratch_shapes=[pltpu.VMEM((GLA_HEADS, dk, dv), F32)],
        compiler_params=_params(("arbitrary", "arbitrary")),
        name="gla_merge",
    )(qkv, qkv, qkv, log_g, gate_a, y_b, norm_w)


def _out_proj_kernel(x_ref, m_ref, w_ref, o_ref):
    o_ref[...] = x_ref[...] + _dot(m_ref[...], w_ref[...])


def _out_proj(x, merged, w, tm=512):
    m, d = x.shape
    row = pl.BlockSpec((tm, d), lambda i: (i, 0))
    return pl.pallas_call(
        _out_proj_kernel,
        out_shape=jax.ShapeDtypeStruct((m, d), F32),
        grid=(m // tm,),
        in_specs=[row, row, pl.BlockSpec((d, d), lambda i: (0, 0))],
        out_specs=row,
        compiler_params=_params(("arbitrary",)),
        name="out_proj",
    )(x, merged, w)


def _ffn_kernel(h_ref, nw_ref, wg_ref, wu_ref, wd_ref, fw_ref, o_ref, hn_ref):
    f = pl.program_id(1)

    @pl.when(f == 0)
    def _():
        h = h_ref[...]
        y = h * lax.rsqrt(jnp.mean(h * h, axis=-1, keepdims=True) + EPS)
        hn_ref[...] = (y * nw_ref[...]).astype(BF16)

    hn = hn_ref[...]
    gate = _dot(hn, wg_ref[...])
    up = _dot(hn, wu_ref[...])
    act = (gate * _sigmoid(gate) * up).astype(BF16)
    part = _dot(act, wd_ref[...])

    @pl.when(f == 0)
    def _():
        o_ref[...] = part

    @pl.when(f != 0)
    def _():
        o_ref[...] += part

    @pl.when(f == pl.num_programs(1) - 1)
    def _():
        z = h_ref[...] + o_ref[...]
        y = z * lax.rsqrt(jnp.mean(z * z, axis=-1, keepdims=True) + EPS)
        o_ref[...] = y * fw_ref[...]


def _ffn(h1, norm_w, wg, wu, wd, final_w, tm=512, tf=512):
    m, d = h1.shape
    hidden = wg.shape[1]
    row = pl.BlockSpec((tm, d), lambda i, f: (i, 0))
    vec = pl.BlockSpec((1, d), lambda i, f: (0, 0))
    return pl.pallas_call(
        _ffn_kernel,
        out_shape=jax.ShapeDtypeStruct((m, d), F32),
        grid=(m // tm, hidden // tf),
        in_specs=[row, vec,
                  pl.BlockSpec((d, tf), lambda i, f: (0, f)),
                  pl.BlockSpec((d, tf), lambda i, f: (0, f)),
                  pl.BlockSpec((tf, d), lambda i, f: (f, 0)),
                  vec],
        out_specs=row,
        scratch_shapes=[pltpu.VMEM((tm, d), BF16)],
        compiler_params=_params(("arbitrary", "arbitrary")),
        name="ffn",
    )(h1, norm_w.reshape(1, d), wg, wu, wd, final_w.reshape(1, d))


def _mixer_layer(x2, batch, seq, mix_norm_w, w_in, w_gk_up, b_gk_up, gla_norm_w,
                 conv_w, w_out, ffn_norm_w, w_gate_up, w_down, final_w):
    d = x2.shape[1]
    kd = w_gk_up.shape[1]
    vd = d
    hidden = w_down.shape[0]

    o_q, o_k, o_v = 0, kd, 2 * kd
    o_go = o_v + vd
    o_gk = o_go + vd
    o_b = o_gk + GK_RANK
    o_c = o_b + d
    o_x = o_c + d
    o_ma = o_x + d
    o_mb = o_ma + d
    wb16 = w_in.astype(BF16)
    w_qkv = wb16[:, o_q:o_go]
    w_go = wb16[:, o_go:o_gk]
    w_gk = jnp.pad(wb16[:, o_gk:o_b], ((0, 0), (0, LANES - GK_RANK)))
    w_b = wb16[:, o_b:o_c]
    w_c = wb16[:, o_c:o_x]
    w_x = wb16[:, o_x:o_ma]
    w_ma = wb16[:, o_ma:o_mb]
    w_mb = wb16[:, o_mb:o_mb + d]
    w_up = jnp.pad(w_gk_up.astype(BF16), ((0, LANES - GK_RANK), (0, 0)))
    conv_w8 = jnp.pad(conv_w, ((0, SUBLANES - CONV_K), (0, 0)))

    hn = _rmsnorm(x2, mix_norm_w)
    qkv = _in_plain(hn, w_qkv)
    gate_a = _in_gate_a(hn, w_go, w_ma)
    y_b = _in_conv(hn, w_b, w_c, w_x, w_mb, conv_w8, seq)
    log_g = _in_decay(hn, w_gk, w_up, b_gk_up.reshape(1, kd))
    merged = _gla_merge(qkv, log_g, gate_a, y_b, gla_norm_w.reshape(1, -1),
                        batch, seq)
    h1 = _out_proj(x2, merged, w_out.astype(BF16))
    wgu = w_gate_up.astype(BF16)
    return _ffn(h1, ffn_norm_w, wgu[:, :hidden], wgu[:, hidden:],
                w_down.astype(BF16), final_w)


def kernel(x, mix_norm_w, w_in, w_gk_up, b_gk_up, gla_norm_w, conv_w, w_out,
           ffn_norm_w, w_gate_up, w_down, final_norm_w):
    batch, seq, d = x.shape
    assert mix_norm_w.shape[0] == 1, "single-layer block"
    out = _mixer_layer(x.reshape(batch * seq, d), batch, seq, mix_norm_w[0],
                       w_in[0], w_gk_up[0], b_gk_up[0], gla_norm_w[0], conv_w[0],
                       w_out[0], ffn_norm_w[0], w_gate_up[0], w_down[0],
                       final_norm_w)
    return out.reshape(batch, seq, d)
```

```python
import functools

import jax
import jax.numpy as jnp
from jax import lax
from jax.experimental import pallas as pl
from jax.experimental.pallas import tpu as pltpu

EPS = 1e-6
GLA_HEADS = 4
GK_RANK = 16
GATE_LOGIT_NORMALIZER = 16.0
CHUNK = 64
CUM_ROWS = 2 * CHUNK
CONV_K = 3

LANES = 128
SUBLANES = 8
VMEM_LIMIT = 56 * 1024 * 1024

F32 = jnp.float32
BF16 = jnp.bfloat16


def _params(semantics):
    return pltpu.CompilerParams(dimension_semantics=semantics,
                                vmem_limit_bytes=VMEM_LIMIT)


def _sigmoid(x):
    return 1.0 / (1.0 + jnp.exp(-x))


def _dot(a, b):
    return jnp.dot(a, b, preferred_element_type=F32)


def _col_blocks(d, tn, col0):
    assert col0 % tn == 0
    return pl.BlockSpec((d, tn), lambda i, j: (0, j + col0 // tn))


def _rmsnorm_kernel(x_ref, w_ref, o_ref):
    x = x_ref[...]
    y = x * lax.rsqrt(jnp.mean(x * x, axis=-1, keepdims=True) + EPS)
    o_ref[...] = (y * w_ref[...]).astype(o_ref.dtype)


def _rmsnorm(x, w, tm=512):
    m, d = x.shape
    return pl.pallas_call(
        _rmsnorm_kernel,
        out_shape=jax.ShapeDtypeStruct((m, d), BF16),
        grid=(m // tm,),
        in_specs=[pl.BlockSpec((tm, d), lambda i: (i, 0)),
                  pl.BlockSpec((1, d), lambda i: (0, 0))],
        out_specs=pl.BlockSpec((tm, d), lambda i: (i, 0)),
        compiler_params=_params(("arbitrary",)),
        name="rmsnorm",
    )(x, w.reshape(1, d))


def _plain_kernel(h_ref, w_ref, o_ref):
    o_ref[...] = _dot(h_ref[...], w_ref[...]).astype(o_ref.dtype)


def _in_plain(hn, w, n, tm=1024, tn=1024):
    m, d = hn.shape
    return pl.pallas_call(
        _plain_kernel,
        out_shape=jax.ShapeDtypeStruct((m, n), BF16),
        grid=(m // tm, n // tn),
        in_specs=[pl.BlockSpec((tm, d), lambda i, j: (i, 0)),
                  _col_blocks(d, tn, 0)],
        out_specs=pl.BlockSpec((tm, tn), lambda i, j: (i, j)),
        compiler_params=_params(("arbitrary", "arbitrary")),
        name="in_plain",
    )(hn, w)


def _gate_a_kernel(h_ref, wg_ref, wm_ref, o_ref):
    h = h_ref[...]
    g = _dot(h, wg_ref[...])
    m = _dot(h, wm_ref[...])
    o_ref[...] = (g * _sigmoid(g) * _sigmoid(m)).astype(o_ref.dtype)


def _in_gate_a(hn, wg, g0, wm, m0, n, tm=1024, tn=512):
    m, d = hn.shape
    return pl.pallas_call(
        _gate_a_kernel,
        out_shape=jax.ShapeDtypeStruct((m, n), BF16),
        grid=(m // tm, n // tn),
        in_specs=[pl.BlockSpec((tm, d), lambda i, j: (i, 0)),
                  _col_blocks(d, tn, g0), _col_blocks(d, tn, m0)],
        out_specs=pl.BlockSpec((tm, tn), lambda i, j: (i, j)),
        compiler_params=_params(("arbitrary", "arbitrary")),
        name="in_gate_a",
    )(hn, wg, wm)


def _conv_kernel(h_ref, wb_ref, wc_ref, wx_ref, wm_ref, cw_ref, o_ref,
                 work_ref, halo_ref, *, tm, tiles_per_seq):
    i = pl.program_id(0)
    j = pl.program_id(1)
    h = h_ref[...]
    u = _dot(h, wc_ref[...]) * _dot(h, wx_ref[...])

    @pl.when(i % tiles_per_seq == 0)
    def _():
        work_ref[0:SUBLANES, :] = jnp.zeros((SUBLANES, u.shape[1]), F32)

    @pl.when(i % tiles_per_seq != 0)
    def _():
        work_ref[0:SUBLANES, :] = halo_ref[j]

    work_ref[SUBLANES:SUBLANES + tm, :] = u
    halo_ref[j] = u[tm - SUBLANES:, :]

    u1 = work_ref[SUBLANES - 1:SUBLANES - 1 + tm, :]
    u2 = work_ref[SUBLANES - 2:SUBLANES - 2 + tm, :]
    cw = cw_ref[...]
    conv = cw[0:1, :] * u2 + cw[1:2, :] * u1 + cw[2:3, :] * u
    y_b = _dot(h, wb_ref[...]) * conv
    o_ref[...] = (_sigmoid(_dot(h, wm_ref[...])) * y_b).astype(o_ref.dtype)


def _in_conv(hn, w, b0, c0, x0, m0, n, conv_w, seq, tm=1024, tn=512):
    m, d = hn.shape
    kern = functools.partial(_conv_kernel, tm=tm, tiles_per_seq=seq // tm)
    return pl.pallas_call(
        kern,
        out_shape=jax.ShapeDtypeStruct((m, n), BF16),
        grid=(m // tm, n // tn),
        in_specs=[pl.BlockSpec((tm, d), lambda i, j: (i, 0)),
                  _col_blocks(d, tn, b0), _col_blocks(d, tn, c0),
                  _col_blocks(d, tn, x0), _col_blocks(d, tn, m0),
                  pl.BlockSpec((SUBLANES, tn), lambda i, j: (0, j))],
        out_specs=pl.BlockSpec((tm, tn), lambda i, j: (i, j)),
        scratch_shapes=[pltpu.VMEM((tm + SUBLANES, tn), F32),
                        pltpu.VMEM((n // tn, SUBLANES, tn), F32)],
        compiler_params=_params(("arbitrary", "arbitrary")),
        name="in_conv",
    )(hn, w, w, w, w, conv_w)


def _decay_kernel(h_ref, wgk_ref, wup_ref, b_ref, o_ref, *, tm):
    low = _dot(h_ref[...], wgk_ref[...]).astype(BF16)
    gk = _dot(low, wup_ref[...]) + b_ref[...]
    log_sig = jnp.minimum(gk, 0.0) - jnp.log1p(jnp.exp(-jnp.abs(gk)))
    log_g = log_sig / GATE_LOGIT_NORMALIZER
    row = lax.broadcasted_iota(jnp.int32, (CUM_ROWS, CUM_ROWS), 0)
    col = lax.broadcasted_iota(jnp.int32, (CUM_ROWS, CUM_ROWS), 1)
    tri = ((row >= col) & (row // CHUNK == col // CHUNK)).astype(BF16)
    for r in range(0, tm, CUM_ROWS):
        g = log_g[r:r + CUM_ROWS, :]
        g1 = g.astype(BF16)
        rem = g - g1.astype(F32)
        g2 = rem.astype(BF16)
        g3 = (rem - g2.astype(F32)).astype(BF16)
        o_ref[r:r + CUM_ROWS, :] = _dot(tri, g1) + _dot(tri, g2) + _dot(tri, g3)


def _in_decay(hn, wgk, wup, b, tm=512):
    m, d = hn.shape
    r = wgk.shape[1]
    n = wup.shape[1]
    return pl.pallas_call(
        functools.partial(_decay_kernel, tm=tm),
        out_shape=jax.ShapeDtypeStruct((m, n), F32),
        grid=(m // tm,),
        in_specs=[pl.BlockSpec((tm, d), lambda i: (i, 0)),
                  pl.BlockSpec((d, r), lambda i: (0, 0)),
                  pl.BlockSpec((r, n), lambda i: (0, 0)),
                  pl.BlockSpec((1, n), lambda i: (0, 0))],
        out_specs=pl.BlockSpec((tm, n), lambda i: (i, 0)),
        compiler_params=_params(("arbitrary",)),
        name="in_decay",
    )(hn, wgk, wup, b)


def _gla_kernel(q_ref, k_ref, v_ref, b_ref, ga_ref, yb_ref, nw_ref, o_ref,
                state_ref, *, rows, dk, dv):
    @pl.when(pl.program_id(1) == 0)
    def _():
        state_ref[...] = jnp.zeros(state_ref.shape, F32)

    row = lax.broadcasted_iota(jnp.int32, (CHUNK, CHUNK), 0)
    col = lax.broadcasted_iota(jnp.int32, (CHUNK, CHUNK), 1)
    causal = row >= col
    scale = dk ** -0.5
    nw = nw_ref[...]

    def chunk_body(c, carry):
        r0 = pl.multiple_of(c * CHUNK, CHUNK)
        rs = pl.ds(r0, CHUNK)
        for h in range(GLA_HEADS):
            ks = slice(h * dk, (h + 1) * dk)
            vs = slice(h * dv, (h + 1) * dv)
            bcum = b_ref[rs, ks]
            b_last = bcum[CHUNK - 1:CHUNK, :]
            q = q_ref[rs, ks].astype(F32)
            k = k_ref[rs, ks].astype(F32)
            q_dec = (q * (scale * jnp.exp(bcum))).astype(BF16)
            k_dec = (k * jnp.exp(-bcum)).astype(BF16)
            k_end = (k * jnp.exp(b_last - bcum)).astype(BF16)
            decay_col = jnp.transpose(
                jnp.broadcast_to(jnp.exp(b_last), (LANES, dk)))
            v = v_ref[rs, vs]
            state = state_ref[h]
            scores = lax.dot_general(q_dec, k_dec, (((1,), (1,)), ((), ())),
                                     preferred_element_type=F32)
            scores = jnp.where(causal, scores, 0.0).astype(BF16)
            o = _dot(scores, v) + _dot(q_dec, state.astype(BF16))
            kv = lax.dot_general(k_end, v, (((0,), (0,)), ((), ())),
                                 preferred_element_type=F32)
            decay = jnp.concatenate([decay_col] * (dv // LANES), axis=1)
            state_ref[h] = state * decay + kv
            y = o * lax.rsqrt(jnp.mean(o * o, axis=-1, keepdims=True) + EPS) * nw
            merged = y * ga_ref[rs, vs].astype(F32) + yb_ref[rs, vs].astype(F32)
            o_ref[rs, vs] = merged.astype(o_ref.dtype)
        return carry

    lax.fori_loop(0, rows // CHUNK, chunk_body, 0)


def _gla_merge(qkv, bcum, gate_a, y_b, norm_w, batch, seq, rows=256):
    m = qkv.shape[0]
    kd = bcum.shape[1]
    vd = gate_a.shape[1]
    dk = kd // GLA_HEADS
    dv = vd // GLA_HEADS
    nblk = seq // rows
    assert qkv.shape[1] == 2 * kd + vd and 2 * kd == vd
    row_map = lambda b, s: (b * nblk + s, 0)
    kern = functools.partial(_gla_kernel, rows=rows, dk=dk, dv=dv)
    return pl.pallas_call(
        kern,
        out_shape=jax.ShapeDtypeStruct((m, vd), BF16),
        grid=(batch, nblk),
        in_specs=[pl.BlockSpec((rows, kd), lambda b, s: (b * nblk + s, 0)),
                  pl.BlockSpec((rows, kd), lambda b, s: (b * nblk + s, 1)),
                  pl.BlockSpec((rows, vd), lambda b, s: (b * nblk + s, 1)),
                  pl.BlockSpec((rows, kd), row_map),
                  pl.BlockSpec((rows, vd), row_map),
                  pl.BlockSpec((rows, vd), row_map),
                  pl.BlockSpec((1, dv), lambda b, s: (0, 0))],
        out_specs=pl.BlockSpec((rows, vd), row_map),
        scratch_shapes=[pltpu.VMEM((GLA_HEADS, dk, dv), F32)],
        compiler_params=_params(("arbitrary", "arbitrary")),
        name="gla_merge",
    )(qkv, qkv, qkv, bcum, gate_a, y_b, norm_w)


def _out_proj_kernel(x_ref, m_ref, w_ref, o_ref):
    o_ref[...] = x_ref[...] + _dot(m_ref[...], w_ref[...])


def _out_proj(x, merged, w, tm=512):
    m, d = x.shape
    row = pl.BlockSpec((tm, d), lambda i: (i, 0))
    return pl.pallas_call(
        _out_proj_kernel,
        out_shape=jax.ShapeDtypeStruct((m, d), F32),
        grid=(m // tm,),
        in_specs=[row, row, pl.BlockSpec((d, d), lambda i: (0, 0))],
        out_specs=row,
        compiler_params=_params(("arbitrary",)),
        name="out_proj",
    )(x, merged, w)


def _ffn_kernel(h_ref, nw_ref, wg_ref, wu_ref, wd_ref, fw_ref, o_ref, hn_ref):
    f = pl.program_id(1)

    @pl.when(f == 0)
    def _():
        h = h_ref[...]
        y = h * lax.rsqrt(jnp.mean(h * h, axis=-1, keepdims=True) + EPS)
        hn_ref[...] = (y * nw_ref[...]).astype(BF16)
        o_ref[...] = h

    hn = hn_ref[...]
    gate = _dot(hn, wg_ref[...])
    up = _dot(hn, wu_ref[...])
    act = (gate * _sigmoid(gate) * up).astype(BF16)
    o_ref[...] += _dot(act, wd_ref[...])

    @pl.when(f == pl.num_programs(1) - 1)
    def _():
        z = o_ref[...]
        y = z * lax.rsqrt(jnp.mean(z * z, axis=-1, keepdims=True) + EPS)
        o_ref[...] = y * fw_ref[...]


def _ffn(h1, norm_w, w_gate_up, wd, final_w, tm=1024, tf=512):
    m, d = h1.shape
    hidden = wd.shape[0]
    nf = hidden // tf
    row = pl.BlockSpec((tm, d), lambda i, f: (i, 0))
    vec = pl.BlockSpec((1, d), lambda i, f: (0, 0))
    return pl.pallas_call(
        _ffn_kernel,
        out_shape=jax.ShapeDtypeStruct((m, d), F32),
        grid=(m // tm, nf),
        in_specs=[row, vec,
                  _col_blocks(d, tf, 0),
                  _col_blocks(d, tf, hidden),
                  pl.BlockSpec((tf, d), lambda i, f: (f, 0)),
                  vec],
        out_specs=row,
        scratch_shapes=[pltpu.VMEM((tm, d), BF16)],
        compiler_params=_params(("arbitrary", "arbitrary")),
        name="ffn",
    )(h1, norm_w.reshape(1, d), w_gate_up, w_gate_up, wd, final_w.reshape(1, d))


def _mixer_layer(x2, batch, seq, mix_norm_w, w_in, w_gk_up, b_gk_up, gla_norm_w,
                 conv_w, w_out, ffn_norm_w, w_gate_up, w_down, final_w):
    d = x2.shape[1]
    kd = w_gk_up.shape[1]
    vd = d

    o_go = 2 * kd + vd
    o_gk = o_go + vd
    o_b = o_gk + GK_RANK
    w_head = w_in[:, :o_gk].astype(BF16)
    w_tail = w_in[:, o_b:].astype(BF16)
    w_gk = jnp.pad(w_in[:, o_gk:o_b].astype(BF16), ((0, 0), (0, LANES - GK_RANK)))
    w_up = jnp.pad(w_gk_up.astype(BF16), ((0, LANES - GK_RANK), (0, 0)))
    conv_w8 = jnp.pad(conv_w, ((0, SUBLANES - CONV_K), (0, 0)))

    hn = _rmsnorm(x2, mix_norm_w)
    qkv = _in_plain(hn, w_head, o_go)
    gate_a = _in_gate_a(hn, w_head, o_go, w_tail, 3 * d, vd)
    y_b = _in_conv(hn, w_tail, 0, d, 2 * d, 4 * d, d, conv_w8, seq)
    bcum = _in_decay(hn, w_gk, w_up, b_gk_up.reshape(1, kd))
    merged = _gla_merge(qkv, bcum, gate_a, y_b, gla_norm_w.reshape(1, -1),
                        batch, seq)
    h1 = _out_proj(x2, merged, w_out.astype(BF16))
    return _ffn(h1, ffn_norm_w, w_gate_up.astype(BF16), w_down.astype(BF16),
                final_w)


def kernel(x, mix_norm_w, w_in, w_gk_up, b_gk_up, gla_norm_w, conv_w, w_out,
           ffn_norm_w, w_gate_up, w_down, final_norm_w):
    batch, seq, d = x.shape
    assert mix_norm_w.shape[0] == 1, "single-layer block"
    out = _mixer_layer(x.reshape(batch * seq, d), batch, seq, mix_norm_w[0],
                       w_in[0], w_gk_up[0], b_gk_up[0], gla_norm_w[0], conv_w[0],
                       w_out[0], ffn_norm_w[0], w_gate_up[0], w_down[0],
                       final_norm_w)
    return out.reshape(batch, seq, d)
```

```python
import functools

import jax
import jax.numpy as jnp
from jax import lax
from jax.experimental import pallas as pl
from jax.experimental.pallas import tpu as pltpu

EPS = 1e-6
GLA_HEADS = 4
GK_RANK = 16
GATE_LOGIT_NORMALIZER = 16.0
CHUNK = 64
CUM_ROWS = 2 * CHUNK
CONV_K = 3
ROW_SUB = 256

LANES = 128
SUBLANES = 8
VMEM_LIMIT = 56 * 1024 * 1024

F32 = jnp.float32
BF16 = jnp.bfloat16


def _params(semantics):
    return pltpu.CompilerParams(dimension_semantics=semantics,
                                vmem_limit_bytes=VMEM_LIMIT)


def _sigmoid(x):
    return 1.0 / (1.0 + jnp.exp(-x))


def _dot(a, b):
    return jnp.dot(a, b, preferred_element_type=F32)


def _col_blocks(d, tn, col0):
    assert col0 % tn == 0
    return pl.BlockSpec((d, tn), lambda i, j: (0, j + col0 // tn))


def _norm_decay_kernel(x_ref, nw_ref, wgk_ref, wup_ref, b_ref, hn_ref, o_ref, *, tm):
    x = x_ref[...]
    y = x * lax.rsqrt(jnp.mean(x * x, axis=-1, keepdims=True) + EPS)
    hn = (y * nw_ref[...]).astype(BF16)
    hn_ref[...] = hn
    low = _dot(hn, wgk_ref[...]).astype(BF16)
    gk = _dot(low, wup_ref[...]) + b_ref[...]
    log_sig = jnp.minimum(gk, 0.0) - jnp.log(1.0 + jnp.exp(-jnp.abs(gk)))
    log_g = log_sig * (1.0 / GATE_LOGIT_NORMALIZER)
    row = lax.broadcasted_iota(jnp.int32, (CUM_ROWS, CUM_ROWS), 0)
    col = lax.broadcasted_iota(jnp.int32, (CUM_ROWS, CUM_ROWS), 1)
    tri = ((row >= col) & (row // CHUNK == col // CHUNK)).astype(BF16)
    for r in range(0, tm, CUM_ROWS):
        g = log_g[r:r + CUM_ROWS, :]
        g1 = g.astype(BF16)
        rem = g - g1.astype(F32)
        g2 = rem.astype(BF16)
        g3 = (rem - g2.astype(F32)).astype(BF16)
        o_ref[r:r + CUM_ROWS, :] = _dot(tri, g1) + _dot(tri, g2) + _dot(tri, g3)


def _norm_decay(x, norm_w, wgk, wup, b, tm=512):
    m, d = x.shape
    r = wgk.shape[1]
    n = wup.shape[1]
    return pl.pallas_call(
        functools.partial(_norm_decay_kernel, tm=tm),
        out_shape=(jax.ShapeDtypeStruct((m, d), BF16),
                   jax.ShapeDtypeStruct((m, n), F32)),
        grid=(m // tm,),
        in_specs=[pl.BlockSpec((tm, d), lambda i: (i, 0)),
                  pl.BlockSpec((1, d), lambda i: (0, 0)),
                  pl.BlockSpec((d, r), lambda i: (0, 0)),
                  pl.BlockSpec((r, n), lambda i: (0, 0)),
                  pl.BlockSpec((1, n), lambda i: (0, 0))],
        out_specs=(pl.BlockSpec((tm, d), lambda i: (i, 0)),
                   pl.BlockSpec((tm, n), lambda i: (i, 0))),
        compiler_params=_params(("arbitrary",)),
        name="norm_decay",
    )(x, norm_w.reshape(1, d), wgk, wup, b)


def _qkv_kernel(h_ref, w_ref, qk_ref, v_ref, *, nqk):
    j = pl.program_id(1)

    @pl.when(j < nqk)
    def _():
        qk_ref[...] = _dot(h_ref[...], w_ref[...])

    @pl.when(j >= nqk)
    def _():
        v_ref[...] = _dot(h_ref[...], w_ref[...]).astype(v_ref.dtype)


def _in_qkv(hn, w, n_qk, n_v, tm=1024, tn=1024):
    m, d = hn.shape
    nqk, nv = n_qk // tn, n_v // tn
    return pl.pallas_call(
        functools.partial(_qkv_kernel, nqk=nqk),
        out_shape=(jax.ShapeDtypeStruct((m, n_qk), F32),
                   jax.ShapeDtypeStruct((m, n_v), BF16)),
        grid=(m // tm, nqk + nv),
        in_specs=[pl.BlockSpec((tm, d), lambda i, j: (i, 0)),
                  _col_blocks(d, tn, 0)],
        out_specs=(pl.BlockSpec((tm, tn), lambda i, j: (i, jnp.minimum(j, nqk - 1))),
                   pl.BlockSpec((tm, tn), lambda i, j: (i, jnp.maximum(j - nqk, 0)))),
        compiler_params=_params(("arbitrary", "arbitrary")),
        name="in_qkv",
    )(hn, w)


def _gate_a_kernel(h_ref, wg_ref, wm_ref, o_ref, *, tm):
    for r in range(0, tm, ROW_SUB):
        h = h_ref[r:r + ROW_SUB, :]
        g = _dot(h, wg_ref[...])
        m = _dot(h, wm_ref[...])
        o_ref[r:r + ROW_SUB, :] = (g * _sigmoid(g) * _sigmoid(m)).astype(o_ref.dtype)


def _in_gate_a(hn, wg, g0, wm, m0, n, tm=1024, tn=512):
    m, d = hn.shape
    return pl.pallas_call(
        functools.partial(_gate_a_kernel, tm=tm),
        out_shape=jax.ShapeDtypeStruct((m, n), BF16),
        grid=(m // tm, n // tn),
        in_specs=[pl.BlockSpec((tm, d), lambda i, j: (i, 0)),
                  _col_blocks(d, tn, g0), _col_blocks(d, tn, m0)],
        out_specs=pl.BlockSpec((tm, tn), lambda i, j: (i, j)),
        compiler_params=_params(("arbitrary", "arbitrary")),
        name="in_gate_a",
    )(hn, wg, wm)


def _conv_kernel(h_ref, wb_ref, wc_ref, wx_ref, wm_ref, cw_ref, o_ref,
                 work_ref, halo_ref, *, tm, tiles_per_seq):
    i = pl.program_id(0)
    j = pl.program_id(1)

    @pl.when(i % tiles_per_seq == 0)
    def _():
        work_ref[0:SUBLANES, :] = jnp.zeros((SUBLANES, work_ref.shape[1]), F32)

    @pl.when(i % tiles_per_seq != 0)
    def _():
        work_ref[0:SUBLANES, :] = halo_ref[j]

    cw = cw_ref[...]
    for r in range(0, tm, ROW_SUB):
        h = h_ref[r:r + ROW_SUB, :]
        u = _dot(h, wc_ref[...]) * _dot(h, wx_ref[...])
        work_ref[SUBLANES + r:SUBLANES + r + ROW_SUB, :] = u
        u1 = work_ref[SUBLANES - 1 + r:SUBLANES - 1 + r + ROW_SUB, :]
        u2 = work_ref[SUBLANES - 2 + r:SUBLANES - 2 + r + ROW_SUB, :]
        conv = cw[0:1, :] * u2 + cw[1:2, :] * u1 + cw[2:3, :] * u
        y_b = _dot(h, wb_ref[...]) * conv
        o_ref[r:r + ROW_SUB, :] = (
            _sigmoid(_dot(h, wm_ref[...])) * y_b).astype(o_ref.dtype)
    halo_ref[j] = work_ref[tm:tm + SUBLANES, :]


def _in_conv(hn, w, b0, c0, x0, m0, n, conv_w, seq, tm=1024, tn=512):
    m, d = hn.shape
    kern = functools.partial(_conv_kernel, tm=tm, tiles_per_seq=seq // tm)
    return pl.pallas_call(
        kern,
        out_shape=jax.ShapeDtypeStruct((m, n), BF16),
        grid=(m // tm, n // tn),
        in_specs=[pl.BlockSpec((tm, d), lambda i, j: (i, 0)),
                  _col_blocks(d, tn, b0), _col_blocks(d, tn, c0),
                  _col_blocks(d, tn, x0), _col_blocks(d, tn, m0),
                  pl.BlockSpec((SUBLANES, tn), lambda i, j: (0, j))],
        out_specs=pl.BlockSpec((tm, tn), lambda i, j: (i, j)),
        scratch_shapes=[pltpu.VMEM((tm + SUBLANES, tn), F32),
                        pltpu.VMEM((n // tn, SUBLANES, tn), F32)],
        compiler_params=_params(("arbitrary", "arbitrary")),
        name="in_conv",
    )(hn, w, w, w, w, conv_w)


def _gla_kernel(q_ref, k_ref, v_ref, b_ref, ga_ref, yb_ref, nw_ref, o_ref,
                state_ref, *, rows, dk, dv):
    @pl.when(pl.program_id(1) == 0)
    def _():
        state_ref[...] = jnp.zeros(state_ref.shape, F32)

    row = lax.broadcasted_iota(jnp.int32, (CHUNK, CHUNK), 0)
    col = lax.broadcasted_iota(jnp.int32, (CHUNK, CHUNK), 1)
    causal = row >= col
    scale = dk ** -0.5
    nw = nw_ref[...]

    def chunk_body(c, carry):
        r0 = pl.multiple_of(c * CHUNK, CHUNK)
        rs = pl.ds(r0, CHUNK)
        heads = range(GLA_HEADS)
        scores, q_decs, k_ends, decay_col = [], [], [], []
        for h in heads:
            ks = slice(h * dk, (h + 1) * dk)
            bcum = b_ref[rs, ks]
            b_last = bcum[CHUNK - 1:CHUNK, :]
            q = q_ref[rs, ks]
            k = k_ref[rs, ks]
            q_dec = (q * (scale * jnp.exp(bcum))).astype(BF16)
            k_dec = (k * jnp.exp(-bcum)).astype(BF16)
            k_ends.append((k * jnp.exp(b_last - bcum)).astype(BF16))
            q_decs.append(q_dec)
            scores.append(lax.dot_general(q_dec, k_dec, (((1,), (1,)), ((), ())),
                                          preferred_element_type=F32))
            decay_col.append(jnp.transpose(
                jnp.broadcast_to(jnp.exp(b_last), (LANES, dk))))
        for h in heads:
            vs = slice(h * dv, (h + 1) * dv)
            v = v_ref[rs, vs]
            o_inter = _dot(q_decs[h], state_ref[h].astype(BF16))
            kv = lax.dot_general(k_ends[h], v, (((0,), (0,)), ((), ())),
                                 preferred_element_type=F32)
            p = jnp.where(causal, scores[h], 0.0).astype(BF16)
            o = _dot(p, v) + o_inter
            decay = jnp.concatenate([decay_col[h]] * (dv // LANES), axis=1)
            state_ref[h] = state_ref[h] * decay + kv
            y = o * lax.rsqrt(jnp.mean(o * o, axis=-1, keepdims=True) + EPS) * nw
            merged = y * ga_ref[rs, vs].astype(F32) + yb_ref[rs, vs].astype(F32)
            o_ref[rs, vs] = merged.astype(o_ref.dtype)
        return carry

    lax.fori_loop(0, rows // CHUNK, chunk_body, 0, unroll=True)


def _gla_merge(qk, v, bcum, gate_a, y_b, norm_w, batch, seq, rows=256):
    m, vd = v.shape
    kd = bcum.shape[1]
    dk = kd // GLA_HEADS
    dv = vd // GLA_HEADS
    nblk = seq // rows
    assert qk.shape[1] == 2 * kd
    row_map = lambda b, s: (b * nblk + s, 0)
    kern = functools.partial(_gla_kernel, rows=rows, dk=dk, dv=dv)
    return pl.pallas_call(
        kern,
        out_shape=jax.ShapeDtypeStruct((m, vd), BF16),
        grid=(batch, nblk),
        in_specs=[pl.BlockSpec((rows, kd), lambda b, s: (b * nblk + s, 0)),
                  pl.BlockSpec((rows, kd), lambda b, s: (b * nblk + s, 1)),
                  pl.BlockSpec((rows, vd), row_map),
                  pl.BlockSpec((rows, kd), row_map),
                  pl.BlockSpec((rows, vd), row_map),
                  pl.BlockSpec((rows, vd), row_map),
                  pl.BlockSpec((1, dv), lambda b, s: (0, 0))],
        out_specs=pl.BlockSpec((rows, vd), row_map),
        scratch_shapes=[pltpu.VMEM((GLA_HEADS, dk, dv), F32)],
        compiler_params=_params(("arbitrary", "arbitrary")),
        name="gla_merge",
    )(qk, qk, v, bcum, gate_a, y_b, norm_w)


def _out_proj_kernel(x_ref, m_ref, w_ref, o_ref):
    o_ref[...] = x_ref[...] + _dot(m_ref[...], w_ref[...])


def _out_proj(x, merged, w, tm=512):
    m, d = x.shape
    row = pl.BlockSpec((tm, d), lambda i: (i, 0))
    return pl.pallas_call(
        _out_proj_kernel,
        out_shape=jax.ShapeDtypeStruct((m, d), F32),
        grid=(m // tm,),
        in_specs=[row, row, pl.BlockSpec((d, d), lambda i: (0, 0))],
        out_specs=row,
        compiler_params=_params(("arbitrary",)),
        name="out_proj",
    )(x, merged, w)


def _ffn_kernel(h_ref, nw_ref, wg_ref, wu_ref, wd_ref, fw_ref, o_ref, hn_ref):
    f = pl.program_id(1)

    @pl.when(f == 0)
    def _():
        h = h_ref[...]
        y = h * lax.rsqrt(jnp.mean(h * h, axis=-1, keepdims=True) + EPS)
        hn_ref[...] = (y * nw_ref[...]).astype(BF16)
        o_ref[...] = h

    hn = hn_ref[...]
    gate = _dot(hn, wg_ref[...])
    up = _dot(hn, wu_ref[...])
    act = (gate * _sigmoid(gate) * up).astype(BF16)
    o_ref[...] += _dot(act, wd_ref[...])

    @pl.when(f == pl.num_programs(1) - 1)
    def _():
        z = o_ref[...]
        y = z * lax.rsqrt(jnp.mean(z * z, axis=-1, keepdims=True) + EPS)
        o_ref[...] = y * fw_ref[...]


def _ffn(h1, norm_w, w_gate_up, wd, final_w, tm=1024, tf=512):
    m, d = h1.shape
    hidden = wd.shape[0]
    nf = hidden // tf
    row = pl.BlockSpec((tm, d), lambda i, f: (i, 0))
    vec = pl.BlockSpec((1, d), lambda i, f: (0, 0))
    return pl.pallas_call(
        _ffn_kernel,
        out_shape=jax.ShapeDtypeStruct((m, d), F32),
        grid=(m // tm, nf),
        in_specs=[row, vec,
                  _col_blocks(d, tf, 0),
                  _col_blocks(d, tf, hidden),
                  pl.BlockSpec((tf, d), lambda i, f: (f, 0)),
                  vec],
        out_specs=row,
        scratch_shapes=[pltpu.VMEM((tm, d), BF16)],
        compiler_params=_params(("arbitrary", "arbitrary")),
        name="ffn",
    )(h1, norm_w.reshape(1, d), w_gate_up, w_gate_up, wd, final_w.reshape(1, d))


def _mixer_layer(x2, batch, seq, mix_norm_w, w_in, w_gk_up, b_gk_up, gla_norm_w,
                 conv_w, w_out, ffn_norm_w, w_gate_up, w_down, final_w):
    d = x2.shape[1]
    kd = w_gk_up.shape[1]
    vd = d

    o_go = 2 * kd + vd
    o_gk = o_go + vd
    o_b = o_gk + GK_RANK
    w_head = w_in[:, :o_gk].astype(BF16)
    w_tail = w_in[:, o_b:].astype(BF16)
    w_gk = jnp.pad(w_in[:, o_gk:o_b].astype(BF16), ((0, 0), (0, LANES - GK_RANK)))
    w_up = jnp.pad(w_gk_up.astype(BF16), ((0, LANES - GK_RANK), (0, 0)))
    conv_w8 = jnp.pad(conv_w, ((0, SUBLANES - CONV_K), (0, 0)))

    hn, bcum = _norm_decay(x2, mix_norm_w, w_gk, w_up, b_gk_up.reshape(1, kd))
    qk, v = _in_qkv(hn, w_head, 2 * kd, vd)
    gate_a = _in_gate_a(hn, w_head, o_go, w_tail, 3 * d, vd)
    y_b = _in_conv(hn, w_tail, 0, d, 2 * d, 4 * d, d, conv_w8, seq)
    merged = _gla_merge(qk, v, bcum, gate_a, y_b, gla_norm_w.reshape(1, -1),
                        batch, seq)
    h1 = _out_proj(x2, merged, w_out.astype(BF16))
    return _ffn(h1, ffn_norm_w, w_gate_up.astype(BF16), w_down.astype(BF16),
                final_w)


def kernel(x, mix_norm_w, w_in, w_gk_up, b_gk_up, gla_norm_w, conv_w, w_out,
           ffn_norm_w, w_gate_up, w_down, final_norm_w):
    batch, seq, d = x.shape
    assert mix_norm_w.shape[0] == 1, "single-layer block"
    out = _mixer_layer(x.reshape(batch * seq, d), batch, seq, mix_norm_w[0],
                       w_in[0], w_gk_up[0], b_gk_up[0], gla_norm_w[0], conv_w[0],
                       w_out[0], ffn_norm_w[0], w_gate_up[0], w_down[0],
                       final_norm_w)
    return out.reshape(batch, seq, d)
```

```python
import functools

import jax
import jax.numpy as jnp
from jax import lax
from jax.experimental import pallas as pl
from jax.experimental.pallas import tpu as pltpu

EPS = 1e-6
GLA_HEADS = 4
GK_RANK = 16
GATE_LOGIT_NORMALIZER = 16.0
CHUNK = 64
CUM_ROWS = 2 * CHUNK
CONV_K = 3
ROW_SUB = 256
ROW_SUB_LAST = 128

LANES = 128
SUBLANES = 8
VMEM_LIMIT = 56 * 1024 * 1024

F32 = jnp.float32
BF16 = jnp.bfloat16


def _params(semantics):
    return pltpu.CompilerParams(dimension_semantics=semantics,
                                vmem_limit_bytes=VMEM_LIMIT)


def _sigmoid(x):
    return 1.0 / (1.0 + jnp.exp(-x))


def _dot(a, b):
    return jnp.dot(a, b, preferred_element_type=F32)


def _col_blocks(d, tn, col0):
    assert col0 % tn == 0
    return pl.BlockSpec((d, tn), lambda i, j: (0, j + col0 // tn))


def _norm_decay_kernel(x_ref, nw_ref, wgk_ref, wup_ref, b_ref, hn_ref, o_ref, *, tm):
    x = x_ref[...]
    y = x * lax.rsqrt(jnp.mean(x * x, axis=-1, keepdims=True) + EPS)
    hn = (y * nw_ref[...]).astype(BF16)
    hn_ref[...] = hn
    low = lax.dot_general(hn, wgk_ref[...], (((1,), (1,)), ((), ())),
                          preferred_element_type=F32).astype(BF16)
    gk = _dot(low, wup_ref[...]) + b_ref[...]
    log_sig = jnp.minimum(gk, 0.0) - jnp.log(1.0 + jnp.exp(-jnp.abs(gk)))
    log_g = log_sig * (1.0 / GATE_LOGIT_NORMALIZER)
    row = lax.broadcasted_iota(jnp.int32, (CUM_ROWS, CUM_ROWS), 0)
    col = lax.broadcasted_iota(jnp.int32, (CUM_ROWS, CUM_ROWS), 1)
    tri = ((row >= col) & (row // CHUNK == col // CHUNK)).astype(BF16)
    for r in range(0, tm, CUM_ROWS):
        g = log_g[r:r + CUM_ROWS, :]
        g1 = g.astype(BF16)
        rem = g - g1.astype(F32)
        g2 = rem.astype(BF16)
        g3 = (rem - g2.astype(F32)).astype(BF16)
        o_ref[r:r + CUM_ROWS, :] = _dot(tri, g1) + _dot(tri, g2) + _dot(tri, g3)


def _norm_decay(x, norm_w, wgk, wup, b, tm=512):
    m, d = x.shape
    r = wgk.shape[0]
    n = wup.shape[1]
    return pl.pallas_call(
        functools.partial(_norm_decay_kernel, tm=tm),
        out_shape=(jax.ShapeDtypeStruct((m, d), BF16),
                   jax.ShapeDtypeStruct((m, n), F32)),
        grid=(m // tm,),
        in_specs=[pl.BlockSpec((tm, d), lambda i: (i, 0)),
                  pl.BlockSpec((1, d), lambda i: (0, 0)),
                  pl.BlockSpec((r, d), lambda i: (0, 0)),
                  pl.BlockSpec((r, n), lambda i: (0, 0)),
                  pl.BlockSpec((1, n), lambda i: (0, 0))],
        out_specs=(pl.BlockSpec((tm, d), lambda i: (i, 0)),
                   pl.BlockSpec((tm, n), lambda i: (i, 0))),
        compiler_params=_params(("arbitrary",)),
        name="norm_decay",
    )(x, norm_w.reshape(1, d), wgk, wup, b)


def _weight_spec(d, tn, col0):
    assert col0 % SUBLANES == 0
    return pl.BlockSpec((pl.Element(tn), pl.Element(d)),
                        lambda j, i: (pl.multiple_of(col0 + j * tn, SUBLANES), 0))


def _load_weight(wt_ref):
    return wt_ref[...].T.astype(BF16)


def _row_subtiles(tm):
    r = 0
    while tm - r > ROW_SUB:
        yield r, ROW_SUB
        r += ROW_SUB
    while r < tm:
        yield r, ROW_SUB_LAST
        r += ROW_SUB_LAST


def _hn_spec(tm, d):
    return pl.BlockSpec((tm, d), lambda j, i: (i, 0))


def _out_spec(tm, tn):
    return pl.BlockSpec((tm, tn), lambda j, i: (i, j))


def _plain_kernel(h_ref, wt_ref, o_ref, wbf_ref):
    @pl.when(pl.program_id(1) == 0)
    def _():
        wbf_ref[...] = _load_weight(wt_ref)

    o_ref[...] = _dot(h_ref[...], wbf_ref[...]).astype(o_ref.dtype)


def _in_plain(hn, wt, col0, n, out_dtype, tm=2048, tn=512):
    m, d = hn.shape
    return pl.pallas_call(
        _plain_kernel,
        out_shape=jax.ShapeDtypeStruct((m, n), out_dtype),
        grid=(n // tn, m // tm),
        in_specs=[_hn_spec(tm, d), _weight_spec(d, tn, col0)],
        out_specs=_out_spec(tm, tn),
        scratch_shapes=[pltpu.VMEM((d, tn), BF16)],
        compiler_params=_params(("arbitrary", "arbitrary")),
        name="in_plain",
    )(hn, wt)


def _gate_a_kernel(h_ref, gt_ref, mt_ref, o_ref, wg_ref, wm_ref, *, tm):
    @pl.when(pl.program_id(1) == 0)
    def _():
        wg_ref[...] = _load_weight(gt_ref)
        wm_ref[...] = _load_weight(mt_ref)

    for r, n in _row_subtiles(tm):
        h = h_ref[r:r + n, :]
        g = _dot(h, wg_ref[...])
        m = _dot(h, wm_ref[...])
        o_ref[r:r + n, :] = (g * _sigmoid(g) * _sigmoid(m)).astype(o_ref.dtype)


def _in_gate_a(hn, wt, g0, m0, n, tm=2048, tn=512):
    m, d = hn.shape
    return pl.pallas_call(
        functools.partial(_gate_a_kernel, tm=tm),
        out_shape=jax.ShapeDtypeStruct((m, n), BF16),
        grid=(n // tn, m // tm),
        in_specs=[_hn_spec(tm, d), _weight_spec(d, tn, g0), _weight_spec(d, tn, m0)],
        out_specs=_out_spec(tm, tn),
        scratch_shapes=[pltpu.VMEM((d, tn), BF16)] * 2,
        compiler_params=_params(("arbitrary", "arbitrary")),
        name="in_gate_a",
    )(hn, wt, wt)


def _conv_kernel(h_ref, bt_ref, ct_ref, xt_ref, mt_ref, cw_ref, o_ref,
                 wb_ref, wc_ref, wx_ref, wm_ref, work_ref, halo_ref,
                 *, tm, tiles_per_seq):
    i = pl.program_id(1)

    @pl.when(i == 0)
    def _():
        wb_ref[...] = _load_weight(bt_ref)
        wc_ref[...] = _load_weight(ct_ref)
        wx_ref[...] = _load_weight(xt_ref)
        wm_ref[...] = _load_weight(mt_ref)

    @pl.when(i % tiles_per_seq == 0)
    def _():
        work_ref[0:SUBLANES, :] = jnp.zeros((SUBLANES, work_ref.shape[1]), F32)

    @pl.when(i % tiles_per_seq != 0)
    def _():
        work_ref[0:SUBLANES, :] = halo_ref[...]

    cw = cw_ref[...]
    for r, n in _row_subtiles(tm):
        h = h_ref[r:r + n, :]
        u = _dot(h, wc_ref[...]) * _dot(h, wx_ref[...])
        work_ref[SUBLANES + r:SUBLANES + r + n, :] = u
        u1 = work_ref[SUBLANES - 1 + r:SUBLANES - 1 + r + n, :]
        u2 = work_ref[SUBLANES - 2 + r:SUBLANES - 2 + r + n, :]
        conv = cw[0:1, :] * u2 + cw[1:2, :] * u1 + cw[2:3, :] * u
        y_b = _dot(h, wb_ref[...]) * conv
        o_ref[r:r + n, :] = (
            _sigmoid(_dot(h, wm_ref[...])) * y_b).astype(o_ref.dtype)
    halo_ref[...] = work_ref[tm:tm + SUBLANES, :]


def _in_conv(hn, wt, b0, c0, x0, m0, n, conv_w, seq, tm=2048, tn=256):
    m, d = hn.shape
    kern = functools.partial(_conv_kernel, tm=tm, tiles_per_seq=seq // tm)
    return pl.pallas_call(
        kern,
        out_shape=jax.ShapeDtypeStruct((m, n), BF16),
        grid=(n // tn, m // tm),
        in_specs=[_hn_spec(tm, d)]
                 + [_weight_spec(d, tn, c) for c in (b0, c0, x0, m0)]
                 + [pl.BlockSpec((SUBLANES, tn), lambda j, i: (0, j))],
        out_specs=_out_spec(tm, tn),
        scratch_shapes=[pltpu.VMEM((d, tn), BF16)] * 4
                       + [pltpu.VMEM((tm + SUBLANES, tn), F32),
                          pltpu.VMEM((SUBLANES, tn), F32)],
        compiler_params=_params(("arbitrary", "arbitrary")),
        name="in_conv",
    )(hn, wt, wt, wt, wt, conv_w)


def _gla_kernel(q_ref, k_ref, v_ref, b_ref, ga_ref, yb_ref, nw_ref, o_ref,
                state_ref, *, rows, dk, dv):
    @pl.when(pl.program_id(1) == 0)
    def _():
        state_ref[...] = jnp.zeros(state_ref.shape, F32)

    row = lax.broadcasted_iota(jnp.int32, (CHUNK, CHUNK), 0)
    col = lax.broadcasted_iota(jnp.int32, (CHUNK, CHUNK), 1)
    causal = row >= col
    scale = dk ** -0.5
    nw = nw_ref[...]

    def chunk_body(c, carry):
        r0 = pl.multiple_of(c * CHUNK, CHUNK)
        rs = pl.ds(r0, CHUNK)
        heads = range(GLA_HEADS)
        scores, q_decs, k_ends, decay_col = [], [], [], []
        for h in heads:
            ks = slice(h * dk, (h + 1) * dk)
            bcum = b_ref[rs, ks]
            b_last = bcum[CHUNK - 1:CHUNK, :]
            q = q_ref[rs, ks]
            k = k_ref[rs, ks]
            q_dec = (q * (scale * jnp.exp(bcum))).astype(BF16)
            k_dec = (k * jnp.exp(-bcum)).astype(BF16)
            k_ends.append((k * jnp.exp(b_last - bcum)).astype(BF16))
            q_decs.append(q_dec)
            scores.append(lax.dot_general(q_dec, k_dec, (((1,), (1,)), ((), ())),
                                          preferred_element_type=F32))
            decay_col.append(jnp.transpose(
                jnp.broadcast_to(jnp.exp(b_last), (LANES, dk))))
        for h in heads:
            vs = slice(h * dv, (h + 1) * dv)
            v = v_ref[rs, vs]
            o_inter = _dot(q_decs[h], state_ref[h].astype(BF16))
            kv = lax.dot_general(k_ends[h], v, (((0,), (0,)), ((), ())),
                                 preferred_element_type=F32)
            p = jnp.where(causal, scores[h], 0.0).astype(BF16)
            o = _dot(p, v) + o_inter
            decay = jnp.concatenate([decay_col[h]] * (dv // LANES), axis=1)
            state_ref[h] = state_ref[h] * decay + kv
            y = o * lax.rsqrt(jnp.mean(o * o, axis=-1, keepdims=True) + EPS) * nw
            merged = y * ga_ref[rs, vs].astype(F32) + yb_ref[rs, vs].astype(F32)
            o_ref[rs, vs] = merged.astype(o_ref.dtype)
        return carry

    lax.fori_loop(0, rows // CHUNK, chunk_body, 0, unroll=True)


def _gla_merge(qk, v, bcum, gate_a, y_b, norm_w, batch, seq, rows=256):
    m, vd = v.shape
    kd = bcum.shape[1]
    dk = kd // GLA_HEADS
    dv = vd // GLA_HEADS
    nblk = seq // rows
    assert qk.shape[1] == 2 * kd
    row_map = lambda b, s: (b * nblk + s, 0)
    kern = functools.partial(_gla_kernel, rows=rows, dk=dk, dv=dv)
    return pl.pallas_call(
        kern,
        out_shape=jax.ShapeDtypeStruct((m, vd), BF16),
        grid=(batch, nblk),
        in_specs=[pl.BlockSpec((rows, kd), lambda b, s: (b * nblk + s, 0)),
                  pl.BlockSpec((rows, kd), lambda b, s: (b * nblk + s, 1)),
                  pl.BlockSpec((rows, vd), row_map),
                  pl.BlockSpec((rows, kd), row_map),
                  pl.BlockSpec((rows, vd), row_map),
                  pl.BlockSpec((rows, vd), row_map),
                  pl.BlockSpec((1, dv), lambda b, s: (0, 0))],
        out_specs=pl.BlockSpec((rows, vd), row_map),
        scratch_shapes=[pltpu.VMEM((GLA_HEADS, dk, dv), F32)],
        compiler_params=_params(("arbitrary", "arbitrary")),
        name="gla_merge",
    )(qk, qk, v, bcum, gate_a, y_b, norm_w)


def _out_proj_kernel(x_ref, m_ref, w_ref, o_ref):
    o_ref[...] = x_ref[...] + _dot(m_ref[...], w_ref[...])


def _out_proj(x, merged, w, tm=512):
    m, d = x.shape
    row = pl.BlockSpec((tm, d), lambda i: (i, 0))
    return pl.pallas_call(
        _out_proj_kernel,
        out_shape=jax.ShapeDtypeStruct((m, d), F32),
        grid=(m // tm,),
        in_specs=[row, row, pl.BlockSpec((d, d), lambda i: (0, 0))],
        out_specs=row,
        compiler_params=_params(("arbitrary",)),
        name="out_proj",
    )(x, merged, w)


def _ffn_kernel(h_ref, nw_ref, wg_ref, wu_ref, wd_ref, fw_ref, o_ref, hn_ref):
    f = pl.program_id(1)

    @pl.when(f == 0)
    def _():
        h = h_ref[...]
        y = h * lax.rsqrt(jnp.mean(h * h, axis=-1, keepdims=True) + EPS)
        hn_ref[...] = (y * nw_ref[...]).astype(BF16)
        o_ref[...] = h

    hn = hn_ref[...]
    gate = _dot(hn, wg_ref[...])
    up = _dot(hn, wu_ref[...])
    act = (gate * _sigmoid(gate) * up).astype(BF16)
    o_ref[...] += _dot(act, wd_ref[...])

    @pl.when(f == pl.num_programs(1) - 1)
    def _():
        z = o_ref[...]
        y = z * lax.rsqrt(jnp.mean(z * z, axis=-1, keepdims=True) + EPS)
        o_ref[...] = y * fw_ref[...]


def _ffn(h1, norm_w, w_gate_up, wd, final_w, tm=1024, tf=512):
    m, d = h1.shape
    hidden = wd.shape[0]
    nf = hidden // tf
    row = pl.BlockSpec((tm, d), lambda i, f: (i, 0))
    vec = pl.BlockSpec((1, d), lambda i, f: (0, 0))
    return pl.pallas_call(
        _ffn_kernel,
        out_shape=jax.ShapeDtypeStruct((m, d), F32),
        grid=(m // tm, nf),
        in_specs=[row, vec,
                  _col_blocks(d, tf, 0),
                  _col_blocks(d, tf, hidden),
                  pl.BlockSpec((tf, d), lambda i, f: (f, 0)),
                  vec],
        out_specs=row,
        scratch_shapes=[pltpu.VMEM((tm, d), BF16)],
        compiler_params=_params(("arbitrary", "arbitrary")),
        name="ffn",
    )(h1, norm_w.reshape(1, d), w_gate_up, w_gate_up, wd, final_w.reshape(1, d))


def _mixer_layer(x2, batch, seq, mix_norm_w, w_in, w_gk_up, b_gk_up, gla_norm_w,
                 conv_w, w_out, ffn_norm_w, w_gate_up, w_down, final_w):
    d = x2.shape[1]
    kd = w_gk_up.shape[1]
    vd = d

    o_v = 2 * kd
    o_go = o_v + vd
    o_gk = o_go + vd
    o_b = o_gk + GK_RANK
    o_c, o_x, o_ma, o_mb = o_b + d, o_b + 2 * d, o_b + 3 * d, o_b + 4 * d
    wt = jnp.swapaxes(w_in, 0, 1)
    w_gk = jnp.pad(wt[o_gk:o_b].astype(BF16), ((0, LANES - GK_RANK), (0, 0)))
    w_up = jnp.pad(w_gk_up.astype(BF16), ((0, LANES - GK_RANK), (0, 0)))
    conv_w8 = jnp.pad(conv_w, ((0, SUBLANES - CONV_K), (0, 0)))

    hn, bcum = _norm_decay(x2, mix_norm_w, w_gk, w_up, b_gk_up.reshape(1, kd))
    qk = _in_plain(hn, wt, 0, o_v, F32)
    v = _in_plain(hn, wt, o_v, vd, BF16)
    gate_a = _in_gate_a(hn, wt, o_go, o_ma, vd)
    y_b = _in_conv(hn, wt, o_b, o_c, o_x, o_mb, d, conv_w8, seq)
    merged = _gla_merge(qk, v, bcum, gate_a, y_b, gla_norm_w.reshape(1, -1),
                        batch, seq)
    h1 = _out_proj(x2, merged, w_out.astype(BF16))
    return _ffn(h1, ffn_norm_w, w_gate_up.astype(BF16), w_down.astype(BF16),
                final_w)


def kernel(x, mix_norm_w, w_in, w_gk_up, b_gk_up, gla_norm_w, conv_w, w_out,
           ffn_norm_w, w_gate_up, w_down, final_norm_w):
    batch, seq, d = x.shape
    assert mix_norm_w.shape[0] == 1, "single-layer block"
    out = _mixer_layer(x.reshape(batch * seq, d), batch, seq, mix_norm_w[0],
                       w_in[0], w_gk_up[0], b_gk_up[0], gla_norm_w[0], conv_w[0],
                       w_out[0], ffn_norm_w[0], w_gate_up[0], w_down[0],
                       final_norm_w)
    return out.reshape(batch, seq, d)
```

```python
import functools

import jax
import jax.numpy as jnp
from jax import lax
from jax.experimental import pallas as pl
from jax.experimental.pallas import tpu as pltpu

EPS = 1e-6
GLA_HEADS = 4
GK_RANK = 16
GATE_LOGIT_NORMALIZER = 16.0
CHUNK = 64
CUM_ROWS = 2 * CHUNK
CONV_K = 3
ROW_SUB = 256
ROW_SUB_LAST = 128

LANES = 128
SUBLANES = 8
VMEM_LIMIT = 56 * 1024 * 1024

F32 = jnp.float32
BF16 = jnp.bfloat16


def _params(semantics):
    return pltpu.CompilerParams(dimension_semantics=semantics,
                                vmem_limit_bytes=VMEM_LIMIT)


def _sigmoid(x):
    return 1.0 / (1.0 + jnp.exp(-x))


def _dot(a, b):
    return jnp.dot(a, b, preferred_element_type=F32)


def _col_blocks(d, tn, col0):
    assert col0 % tn == 0
    return pl.BlockSpec((d, tn), lambda i, j: (0, j + col0 // tn))


def _norm_decay_kernel(x_ref, nw_ref, wgk_ref, wup_ref, b_ref, hn_ref, o_ref, *, tm):
    row = lax.broadcasted_iota(jnp.int32, (CUM_ROWS, CUM_ROWS), 0)
    col = lax.broadcasted_iota(jnp.int32, (CUM_ROWS, CUM_ROWS), 1)
    tri = ((row >= col) & (row // CHUNK == col // CHUNK)).astype(BF16)
    for r in range(0, tm, CUM_ROWS):
        rows = slice(r, r + CUM_ROWS)
        x = x_ref[rows, :]
        y = x * lax.rsqrt(jnp.mean(x * x, axis=-1, keepdims=True) + EPS)
        hn = (y * nw_ref[...]).astype(BF16)
        hn_ref[rows, :] = hn
        low = lax.dot_general(hn, wgk_ref[...], (((1,), (1,)), ((), ())),
                              preferred_element_type=F32).astype(BF16)
        gk = _dot(low, wup_ref[...]) + b_ref[...]
        log_sig = jnp.minimum(gk, 0.0) - jnp.log(1.0 + jnp.exp(-jnp.abs(gk)))
        g = log_sig * (1.0 / GATE_LOGIT_NORMALIZER)
        g1 = g.astype(BF16)
        rem = g - g1.astype(F32)
        g2 = rem.astype(BF16)
        g3 = (rem - g2.astype(F32)).astype(BF16)
        o_ref[rows, :] = _dot(tri, g1) + _dot(tri, g2) + _dot(tri, g3)


def _norm_decay(x, norm_w, wgk, wup, b, tm=512):
    m, d = x.shape
    r = wgk.shape[0]
    n = wup.shape[1]
    return pl.pallas_call(
        functools.partial(_norm_decay_kernel, tm=tm),
        out_shape=(jax.ShapeDtypeStruct((m, d), BF16),
                   jax.ShapeDtypeStruct((m, n), F32)),
        grid=(m // tm,),
        in_specs=[pl.BlockSpec((tm, d), lambda i: (i, 0)),
                  pl.BlockSpec((1, d), lambda i: (0, 0)),
                  pl.BlockSpec((r, d), lambda i: (0, 0)),
                  pl.BlockSpec((r, n), lambda i: (0, 0)),
                  pl.BlockSpec((1, n), lambda i: (0, 0))],
        out_specs=(pl.BlockSpec((tm, d), lambda i: (i, 0)),
                   pl.BlockSpec((tm, n), lambda i: (i, 0))),
        compiler_params=_params(("arbitrary",)),
        name="norm_decay",
    )(x, norm_w.reshape(1, d), wgk, wup, b)


def _weight_spec(d, tn, col0):
    assert col0 % SUBLANES == 0
    return pl.BlockSpec((pl.Element(tn), pl.Element(d)),
                        lambda j, i: (pl.multiple_of(col0 + j * tn, SUBLANES), 0))


def _load_weight(wt_ref):
    return wt_ref[...].T.astype(BF16)


def _row_subtiles(tm):
    r = 0
    while tm - r > ROW_SUB:
        yield r, ROW_SUB
        r += ROW_SUB
    while r < tm:
        yield r, ROW_SUB_LAST
        r += ROW_SUB_LAST


def _hn_spec(tm, d):
    return pl.BlockSpec((tm, d), lambda j, i: (i, 0))


def _out_spec(tm, tn):
    return pl.BlockSpec((tm, tn), lambda j, i: (i, j))


def _plain_kernel(h_ref, wt_ref, o_ref, wbf_ref):
    @pl.when(pl.program_id(1) == 0)
    def _():
        wbf_ref[...] = _load_weight(wt_ref)

    o_ref[...] = _dot(h_ref[...], wbf_ref[...]).astype(o_ref.dtype)


def _in_plain(hn, wt, col0, n, out_dtype, tm=1024, tn=1024):
    m, d = hn.shape
    return pl.pallas_call(
        _plain_kernel,
        out_shape=jax.ShapeDtypeStruct((m, n), out_dtype),
        grid=(n // tn, m // tm),
        in_specs=[_hn_spec(tm, d), _weight_spec(d, tn, col0)],
        out_specs=_out_spec(tm, tn),
        scratch_shapes=[pltpu.VMEM((d, tn), BF16)],
        compiler_params=_params(("arbitrary", "arbitrary")),
        name="in_plain",
    )(hn, wt)


def _gate_a_kernel(h_ref, gt_ref, mt_ref, o_ref, wg_ref, wm_ref, *, tm):
    @pl.when(pl.program_id(1) == 0)
    def _():
        wg_ref[...] = _load_weight(gt_ref)
        wm_ref[...] = _load_weight(mt_ref)

    for r, n in _row_subtiles(tm):
        h = h_ref[r:r + n, :]
        g = _dot(h, wg_ref[...])
        m = _dot(h, wm_ref[...])
        o_ref[r:r + n, :] = (g * _sigmoid(g) * _sigmoid(m)).astype(o_ref.dtype)


def _in_gate_a(hn, wt, g0, m0, n, tm=2048, tn=512):
    m, d = hn.shape
    return pl.pallas_call(
        functools.partial(_gate_a_kernel, tm=tm),
        out_shape=jax.ShapeDtypeStruct((m, n), BF16),
        grid=(n // tn, m // tm),
        in_specs=[_hn_spec(tm, d), _weight_spec(d, tn, g0), _weight_spec(d, tn, m0)],
        out_specs=_out_spec(tm, tn),
        scratch_shapes=[pltpu.VMEM((d, tn), BF16)] * 2,
        compiler_params=_params(("arbitrary", "arbitrary")),
        name="in_gate_a",
    )(hn, wt, wt)


def _conv_kernel(h_ref, bt_ref, ct_ref, xt_ref, mt_ref, cw_ref, o_ref,
                 wb_ref, wc_ref, wx_ref, wm_ref, work_ref, halo_ref,
                 *, tm, tiles_per_seq):
    i = pl.program_id(1)

    @pl.when(i == 0)
    def _():
        wb_ref[...] = _load_weight(bt_ref)
        wc_ref[...] = _load_weight(ct_ref)
        wx_ref[...] = _load_weight(xt_ref)
        wm_ref[...] = _load_weight(mt_ref)

    @pl.when(i % tiles_per_seq == 0)
    def _():
        work_ref[0:SUBLANES, :] = jnp.zeros((SUBLANES, work_ref.shape[1]), F32)

    @pl.when(i % tiles_per_seq != 0)
    def _():
        work_ref[0:SUBLANES, :] = halo_ref[...]

    cw = cw_ref[...]
    for r, n in _row_subtiles(tm):
        h = h_ref[r:r + n, :]
        u = _dot(h, wc_ref[...]) * _dot(h, wx_ref[...])
        work_ref[SUBLANES + r:SUBLANES + r + n, :] = u
        u1 = work_ref[SUBLANES - 1 + r:SUBLANES - 1 + r + n, :]
        u2 = work_ref[SUBLANES - 2 + r:SUBLANES - 2 + r + n, :]
        conv = cw[0:1, :] * u2 + cw[1:2, :] * u1 + cw[2:3, :] * u
        y_b = _dot(h, wb_ref[...]) * conv
        o_ref[r:r + n, :] = (
            _sigmoid(_dot(h, wm_ref[...])) * y_b).astype(o_ref.dtype)
    halo_ref[...] = work_ref[tm:tm + SUBLANES, :]


def _in_conv(hn, wt, b0, c0, x0, m0, n, conv_w, seq, tm=2048, tn=256):
    m, d = hn.shape
    kern = functools.partial(_conv_kernel, tm=tm, tiles_per_seq=seq // tm)
    return pl.pallas_call(
        kern,
        out_shape=jax.ShapeDtypeStruct((m, n), BF16),
        grid=(n // tn, m // tm),
        in_specs=[_hn_spec(tm, d)]
                 + [_weight_spec(d, tn, c) for c in (b0, c0, x0, m0)]
                 + [pl.BlockSpec((SUBLANES, tn), lambda j, i: (0, j))],
        out_specs=_out_spec(tm, tn),
        scratch_shapes=[pltpu.VMEM((d, tn), BF16)] * 4
                       + [pltpu.VMEM((tm + SUBLANES, tn), F32),
                          pltpu.VMEM((SUBLANES, tn), F32)],
        compiler_params=_params(("arbitrary", "arbitrary")),
        name="in_conv",
    )(hn, wt, wt, wt, wt, conv_w)


def _gla_kernel(q_ref, k_ref, v_ref, b_ref, ga_ref, yb_ref, nw_ref, o_ref,
                state_ref, *, rows, dk, dv):
    @pl.when(pl.program_id(1) == 0)
    def _():
        state_ref[...] = jnp.zeros(state_ref.shape, F32)

    row = lax.broadcasted_iota(jnp.int32, (CHUNK, CHUNK), 0)
    col = lax.broadcasted_iota(jnp.int32, (CHUNK, CHUNK), 1)
    causal = row >= col
    scale = dk ** -0.5
    nw = nw_ref[...]

    def chunk_body(c, carry):
        r0 = pl.multiple_of(c * CHUNK, CHUNK)
        rs = pl.ds(r0, CHUNK)
        heads = range(GLA_HEADS)
        scores, q_decs, k_ends, decay_col = [], [], [], []
        for h in heads:
            ks = slice(h * dk, (h + 1) * dk)
            bcum = b_ref[rs, ks]
            b_last = bcum[CHUNK - 1:CHUNK, :]
            q = q_ref[rs, ks]
            k = k_ref[rs, ks]
            q_dec = (q * (scale * jnp.exp(bcum))).astype(BF16)
            k_dec = (k * jnp.exp(-bcum)).astype(BF16)
            k_ends.append((k * jnp.exp(b_last - bcum)).astype(BF16))
            q_decs.append(q_dec)
            scores.append(lax.dot_general(q_dec, k_dec, (((1,), (1,)), ((), ())),
                                          preferred_element_type=F32))
            decay_col.append(jnp.transpose(
                jnp.broadcast_to(jnp.exp(b_last), (LANES, dk))))
        for h in heads:
            vs = slice(h * dv, (h + 1) * dv)
            v = v_ref[rs, vs]
            o_inter = _dot(q_decs[h], state_ref[h].astype(BF16))
            kv = lax.dot_general(k_ends[h], v, (((0,), (0,)), ((), ())),
                                 preferred_element_type=F32)
            p = jnp.where(causal, scores[h], 0.0).astype(BF16)
            o = _dot(p, v) + o_inter
            decay = jnp.concatenate([decay_col[h]] * (dv // LANES), axis=1)
            state_ref[h] = state_ref[h] * decay + kv
            y = o * lax.rsqrt(jnp.mean(o * o, axis=-1, keepdims=True) + EPS) * nw
            merged = y * ga_ref[rs, vs].astype(F32) + yb_ref[rs, vs].astype(F32)
            o_ref[rs, vs] = merged.astype(o_ref.dtype)
        return carry

    lax.fori_loop(0, rows // CHUNK, chunk_body, 0, unroll=True)


def _gla_merge(qk, v, bcum, gate_a, y_b, norm_w, batch, seq, rows=256):
    m, vd = v.shape
    kd = bcum.shape[1]
    dk = kd // GLA_HEADS
    dv = vd // GLA_HEADS
    nblk = seq // rows
    assert qk.shape[1] == 2 * kd
    row_map = lambda b, s: (b * nblk + s, 0)
    kern = functools.partial(_gla_kernel, rows=rows, dk=dk, dv=dv)
    return pl.pallas_call(
        kern,
        out_shape=jax.ShapeDtypeStruct((m, vd), BF16),
        grid=(batch, nblk),
        in_specs=[pl.BlockSpec((rows, kd), lambda b, s: (b * nblk + s, 0)),
                  pl.BlockSpec((rows, kd), lambda b, s: (b * nblk + s, 1)),
                  pl.BlockSpec((rows, vd), row_map),
                  pl.BlockSpec((rows, kd), row_map),
                  pl.BlockSpec((rows, vd), row_map),
                  pl.BlockSpec((rows, vd), row_map),
                  pl.BlockSpec((1, dv), lambda b, s: (0, 0))],
        out_specs=pl.BlockSpec((rows, vd), row_map),
        scratch_shapes=[pltpu.VMEM((GLA_HEADS, dk, dv), F32)],
        compiler_params=_params(("arbitrary", "arbitrary")),
        name="gla_merge",
    )(qk, qk, v, bcum, gate_a, y_b, norm_w)


def _out_proj_kernel(x_ref, m_ref, w_ref, o_ref):
    o_ref[...] = x_ref[...] + _dot(m_ref[...], w_ref[...])


def _out_proj(x, merged, w, tm=512):
    m, d = x.shape
    row = pl.BlockSpec((tm, d), lambda i: (i, 0))
    return pl.pallas_call(
        _out_proj_kernel,
        out_shape=jax.ShapeDtypeStruct((m, d), F32),
        grid=(m // tm,),
        in_specs=[row, row, pl.BlockSpec((d, d), lambda i: (0, 0))],
        out_specs=row,
        compiler_params=_params(("arbitrary",)),
        name="out_proj",
    )(x, merged, w)


def _ffn_kernel(h_ref, nw_ref, wg_ref, wu_ref, wd_ref, fw_ref, o_ref, hn_ref, *, tm):
    f = pl.program_id(1)
    last_f = pl.num_programs(1) - 1

    def step(first, last):
        for r in range(0, tm, ROW_SUB):
            rows = slice(r, r + ROW_SUB)
            if first:
                h = h_ref[rows, :]
                y = h * lax.rsqrt(jnp.mean(h * h, axis=-1, keepdims=True) + EPS)
                hn = (y * nw_ref[...]).astype(BF16)
                hn_ref[rows, :] = hn
                acc = h
            else:
                hn = hn_ref[rows, :]
                acc = o_ref[rows, :]
            gate = _dot(hn, wg_ref[...])
            up = _dot(hn, wu_ref[...])
            act = (gate * _sigmoid(gate) * up).astype(BF16)
            z = acc + _dot(act, wd_ref[...])
            if last:
                y = z * lax.rsqrt(jnp.mean(z * z, axis=-1, keepdims=True) + EPS)
                z = y * fw_ref[...]
            o_ref[rows, :] = z

    pl.when(f == 0)(functools.partial(step, True, False))
    pl.when((f > 0) & (f < last_f))(functools.partial(step, False, False))
    pl.when(f == last_f)(functools.partial(step, False, True))


def _ffn(h1, norm_w, w_gate_up, wd, final_w, tm=1024, tf=512):
    m, d = h1.shape
    hidden = wd.shape[0]
    nf = hidden // tf
    assert nf >= 3
    row = pl.BlockSpec((tm, d), lambda i, f: (i, 0))
    vec = pl.BlockSpec((1, d), lambda i, f: (0, 0))
    return pl.pallas_call(
        functools.partial(_ffn_kernel, tm=tm),
        out_shape=jax.ShapeDtypeStruct((m, d), F32),
        grid=(m // tm, nf),
        in_specs=[row, vec,
                  _col_blocks(d, tf, 0),
                  _col_blocks(d, tf, hidden),
                  pl.BlockSpec((tf, d), lambda i, f: (f, 0)),
                  vec],
        out_specs=row,
        scratch_shapes=[pltpu.VMEM((tm, d), BF16)],
        compiler_params=_params(("arbitrary", "arbitrary")),
        name="ffn",
    )(h1, norm_w.reshape(1, d), w_gate_up, w_gate_up, wd, final_w.reshape(1, d))


def _mixer_layer(x2, batch, seq, mix_norm_w, w_in, w_gk_up, b_gk_up, gla_norm_w,
                 conv_w, w_out, ffn_norm_w, w_gate_up, w_down, final_w):
    d = x2.shape[1]
    kd = w_gk_up.shape[1]
    vd = d

    o_v = 2 * kd
    o_go = o_v + vd
    o_gk = o_go + vd
    o_b = o_gk + GK_RANK
    o_c, o_x, o_ma, o_mb = o_b + d, o_b + 2 * d, o_b + 3 * d, o_b + 4 * d
    wt = jnp.swapaxes(w_in, 0, 1)
    w_gk = jnp.pad(wt[o_gk:o_b].astype(BF16), ((0, LANES - GK_RANK), (0, 0)))
    w_up = jnp.pad(w_gk_up.astype(BF16), ((0, LANES - GK_RANK), (0, 0)))
    conv_w8 = jnp.pad(conv_w, ((0, SUBLANES - CONV_K), (0, 0)))

    hn, bcum = _norm_decay(x2, mix_norm_w, w_gk, w_up, b_gk_up.reshape(1, kd))
    qk = _in_plain(hn, wt, 0, o_v, F32)
    v = _in_plain(hn, wt, o_v, vd, BF16)
    gate_a = _in_gate_a(hn, wt, o_go, o_ma, vd)
    y_b = _in_conv(hn, wt, o_b, o_c, o_x, o_mb, d, conv_w8, seq)
    merged = _gla_merge(qk, v, bcum, gate_a, y_b, gla_norm_w.reshape(1, -1),
                        batch, seq)
    h1 = _out_proj(x2, merged, w_out.astype(BF16))
    return _ffn(h1, ffn_norm_w, w_gate_up.astype(BF16), w_down.astype(BF16),
                final_w)


def kernel(x, mix_norm_w, w_in, w_gk_up, b_gk_up, gla_norm_w, conv_w, w_out,
           ffn_norm_w, w_gate_up, w_down, final_norm_w):
    batch, seq, d = x.shape
    assert mix_norm_w.shape[0] == 1, "single-layer block"
    out = _mixer_layer(x.reshape(batch * seq, d), batch, seq, mix_norm_w[0],
                       w_in[0], w_gk_up[0], b_gk_up[0], gla_norm_w[0], conv_w[0],
                       w_out[0], ffn_norm_w[0], w_gate_up[0], w_down[0],
                       final_norm_w)
    return out.reshape(batch, seq, d)
```

```python
import functools

import jax
import jax.numpy as jnp
from jax import lax
from jax.experimental import pallas as pl
from jax.experimental.pallas import tpu as pltpu

EPS = 1e-6
GLA_HEADS = 4
GK_RANK = 16
GATE_LOGIT_NORMALIZER = 16.0
CHUNK = 64
PAIR = 2 * CHUNK
CUM_ROWS = PAIR
CONV_K = 3
ROW_SUB = 256
ROW_SUB_LAST = 128

LANES = 128
SUBLANES = 8
VMEM_LIMIT = 56 * 1024 * 1024

F32 = jnp.float32
BF16 = jnp.bfloat16


def _params(semantics):
    return pltpu.CompilerParams(dimension_semantics=semantics,
                                vmem_limit_bytes=VMEM_LIMIT)


def _sigmoid(x):
    return 1.0 / (1.0 + jnp.exp(-x))


def _dot(a, b):
    return jnp.dot(a, b, preferred_element_type=F32)


def _col_blocks(d, tn, col0):
    assert col0 % tn == 0
    return pl.BlockSpec((d, tn), lambda i, j: (0, j + col0 // tn))


def _norm_decay_kernel(x_ref, nw_ref, wgk_ref, wup_ref, b_ref, hn_ref, o_ref, *, tm):
    row = lax.broadcasted_iota(jnp.int32, (CUM_ROWS, CUM_ROWS), 0)
    col = lax.broadcasted_iota(jnp.int32, (CUM_ROWS, CUM_ROWS), 1)
    tri = ((row >= col) & (row // CHUNK == col // CHUNK)).astype(BF16)
    for r in range(0, tm, CUM_ROWS):
        rows = slice(r, r + CUM_ROWS)
        x = x_ref[rows, :]
        y = x * lax.rsqrt(jnp.mean(x * x, axis=-1, keepdims=True) + EPS)
        hn = (y * nw_ref[...]).astype(BF16)
        hn_ref[rows, :] = hn
        low = lax.dot_general(hn, wgk_ref[...], (((1,), (1,)), ((), ())),
                              preferred_element_type=F32).astype(BF16)
        gk = _dot(low, wup_ref[...]) + b_ref[...]
        log_sig = jnp.minimum(gk, 0.0) - jnp.log(1.0 + jnp.exp(-jnp.abs(gk)))
        g = log_sig * (1.0 / GATE_LOGIT_NORMALIZER)
        g1 = g.astype(BF16)
        rem = g - g1.astype(F32)
        g2 = rem.astype(BF16)
        g3 = (rem - g2.astype(F32)).astype(BF16)
        o_ref[rows, :] = _dot(tri, g1) + _dot(tri, g2) + _dot(tri, g3)


def _norm_decay(x, norm_w, wgk, wup, b, tm=512):
    m, d = x.shape
    r = wgk.shape[0]
    n = wup.shape[1]
    return pl.pallas_call(
        functools.partial(_norm_decay_kernel, tm=tm),
        out_shape=(jax.ShapeDtypeStruct((m, d), BF16),
                   jax.ShapeDtypeStruct((m, n), F32)),
        grid=(m // tm,),
        in_specs=[pl.BlockSpec((tm, d), lambda i: (i, 0)),
                  pl.BlockSpec((1, d), lambda i: (0, 0)),
                  pl.BlockSpec((r, d), lambda i: (0, 0)),
                  pl.BlockSpec((r, n), lambda i: (0, 0)),
                  pl.BlockSpec((1, n), lambda i: (0, 0))],
        out_specs=(pl.BlockSpec((tm, d), lambda i: (i, 0)),
                   pl.BlockSpec((tm, n), lambda i: (i, 0))),
        compiler_params=_params(("arbitrary",)),
        name="norm_decay",
    )(x, norm_w.reshape(1, d), wgk, wup, b)


def _weight_spec(d, tn, col0):
    assert col0 % SUBLANES == 0
    return pl.BlockSpec((pl.Element(tn), pl.Element(d)),
                        lambda j, i: (pl.multiple_of(col0 + j * tn, SUBLANES), 0))


def _load_weight(wt_ref):
    return wt_ref[...].T.astype(BF16)


def _row_subtiles(tm):
    r = 0
    while tm - r > ROW_SUB:
        yield r, ROW_SUB
        r += ROW_SUB
    while r < tm:
        yield r, ROW_SUB_LAST
        r += ROW_SUB_LAST


def _hn_spec(tm, d):
    return pl.BlockSpec((tm, d), lambda j, i: (i, 0))


def _out_spec(tm, tn):
    return pl.BlockSpec((tm, tn), lambda j, i: (i, j))


def _plain_kernel(h_ref, wt_ref, o_ref, wbf_ref):
    @pl.when(pl.program_id(1) == 0)
    def _():
        wbf_ref[...] = _load_weight(wt_ref)

    o_ref[...] = _dot(h_ref[...], wbf_ref[...]).astype(o_ref.dtype)


def _in_plain(hn, wt, col0, n, out_dtype, tm=1024, tn=1024):
    m, d = hn.shape
    return pl.pallas_call(
        _plain_kernel,
        out_shape=jax.ShapeDtypeStruct((m, n), out_dtype),
        grid=(n // tn, m // tm),
        in_specs=[_hn_spec(tm, d), _weight_spec(d, tn, col0)],
        out_specs=_out_spec(tm, tn),
        scratch_shapes=[pltpu.VMEM((d, tn), BF16)],
        compiler_params=_params(("arbitrary", "arbitrary")),
        name="in_plain",
    )(hn, wt)


def _gate_a_kernel(h_ref, gt_ref, mt_ref, nw_ref, o_ref, wg_ref, wm_ref, *, tm):
    @pl.when(pl.program_id(1) == 0)
    def _():
        wg_ref[...] = _load_weight(gt_ref)
        wm_ref[...] = _load_weight(mt_ref)

    for r, n in _row_subtiles(tm):
        h = h_ref[r:r + n, :]
        g = _dot(h, wg_ref[...])
        m = _dot(h, wm_ref[...])
        o_ref[r:r + n, :] = (
            g * _sigmoid(g) * _sigmoid(m) * nw_ref[...]).astype(o_ref.dtype)


def _in_gate_a(hn, wt, g0, m0, n, gain, tm=2048, tn=512):
    m, d = hn.shape
    return pl.pallas_call(
        functools.partial(_gate_a_kernel, tm=tm),
        out_shape=jax.ShapeDtypeStruct((m, n), BF16),
        grid=(n // tn, m // tm),
        in_specs=[_hn_spec(tm, d), _weight_spec(d, tn, g0), _weight_spec(d, tn, m0),
                  pl.BlockSpec((1, tn), lambda j, i: (0, j))],
        out_specs=_out_spec(tm, tn),
        scratch_shapes=[pltpu.VMEM((d, tn), BF16)] * 2,
        compiler_params=_params(("arbitrary", "arbitrary")),
        name="in_gate_a",
    )(hn, wt, wt, gain)


def _conv_kernel(h_ref, bt_ref, ct_ref, xt_ref, mt_ref, cw_ref, o_ref,
                 wb_ref, wc_ref, wx_ref, wm_ref, work_ref, halo_ref,
                 *, tm, tiles_per_seq):
    i = pl.program_id(1)

    @pl.when(i == 0)
    def _():
        wb_ref[...] = _load_weight(bt_ref)
        wc_ref[...] = _load_weight(ct_ref)
        wx_ref[...] = _load_weight(xt_ref)
        wm_ref[...] = _load_weight(mt_ref)

    @pl.when(i % tiles_per_seq == 0)
    def _():
        work_ref[0:SUBLANES, :] = jnp.zeros((SUBLANES, work_ref.shape[1]), F32)

    @pl.when(i % tiles_per_seq != 0)
    def _():
        work_ref[0:SUBLANES, :] = halo_ref[...]

    cw = cw_ref[...]
    for r, n in _row_subtiles(tm):
        h = h_ref[r:r + n, :]
        u = _dot(h, wc_ref[...]) * _dot(h, wx_ref[...])
        work_ref[SUBLANES + r:SUBLANES + r + n, :] = u
        u1 = work_ref[SUBLANES - 1 + r:SUBLANES - 1 + r + n, :]
        u2 = work_ref[SUBLANES - 2 + r:SUBLANES - 2 + r + n, :]
        conv = cw[0:1, :] * u2 + cw[1:2, :] * u1 + cw[2:3, :] * u
        y_b = _dot(h, wb_ref[...]) * conv
        o_ref[r:r + n, :] = (
            _sigmoid(_dot(h, wm_ref[...])) * y_b).astype(o_ref.dtype)
    halo_ref[...] = work_ref[tm:tm + SUBLANES, :]


def _in_conv(hn, wt, b0, c0, x0, m0, n, conv_w, seq, tm=2048, tn=256):
    m, d = hn.shape
    kern = functools.partial(_conv_kernel, tm=tm, tiles_per_seq=seq // tm)
    return pl.pallas_call(
        kern,
        out_shape=jax.ShapeDtypeStruct((m, n), BF16),
        grid=(n // tn, m // tm),
        in_specs=[_hn_spec(tm, d)]
                 + [_weight_spec(d, tn, c) for c in (b0, c0, x0, m0)]
                 + [pl.BlockSpec((SUBLANES, tn), lambda j, i: (0, j))],
        out_specs=_out_spec(tm, tn),
        scratch_shapes=[pltpu.VMEM((d, tn), BF16)] * 4
                       + [pltpu.VMEM((tm + SUBLANES, tn), F32),
                          pltpu.VMEM((SUBLANES, tn), F32)],
        compiler_params=_params(("arbitrary", "arbitrary")),
        name="in_conv",
    )(hn, wt, wt, wt, wt, conv_w)


def _gla_kernel(q_ref, k_ref, v_ref, b_ref, ga_ref, yb_ref, o_ref, state_ref,
                *, rows, dk, dv):
    @pl.when(pl.program_id(1) == 0)
    def _():
        state_ref[...] = jnp.zeros(state_ref.shape, F32)

    row = lax.broadcasted_iota(jnp.int32, (CHUNK, PAIR), 0)
    col = lax.broadcasted_iota(jnp.int32, (CHUNK, PAIR), 1)
    mask_a = (row >= col)[:, :CHUNK]
    mask_b = col <= row + CHUNK
    scale = dk ** -0.5
    nt = (((1,), (1,)), ((), ()))
    tn = (((0,), (0,)), ((), ()))

    def pair_body(c, carry):
        r0 = pl.multiple_of(c * PAIR, PAIR)
        rs = pl.ds(r0, PAIR)
        heads = range(GLA_HEADS)
        s_as, s_bs, q_ints, k_sts, decay_cols = [], [], [], [], []
        for h in heads:
            ks = slice(h * dk, (h + 1) * dk)
            b = b_ref[rs, ks]
            b_last_a = b[CHUNK - 1:CHUNK, :]
            b_last_b = b[PAIR - 1:PAIR, :]
            b_end = jnp.concatenate([jnp.broadcast_to(b_last_a, (CHUNK, dk)),
                                     jnp.broadcast_to(b_last_b, (CHUNK, dk))], axis=0)
            q = q_ref[rs, ks]
            k = k_ref[rs, ks]
            qd = q * (scale * jnp.exp(b))
            ke = k * jnp.exp(b_end - b)
            q_dec = qd.astype(BF16)
            k_dec = (k * jnp.exp(-b)).astype(BF16)
            k_end = ke.astype(BF16)
            q_ints.append(jnp.concatenate(
                [q_dec[:CHUNK], (qd[CHUNK:] * jnp.exp(b_last_a)).astype(BF16)], axis=0))
            k_sts.append(jnp.concatenate(
                [(ke[:CHUNK] * jnp.exp(b_last_b)).astype(BF16), k_end[CHUNK:]], axis=0))
            k_x = jnp.concatenate([k_end[:CHUNK], k_dec[CHUNK:]], axis=0)
            s_as.append(lax.dot_general(q_dec[:CHUNK], k_dec[:CHUNK], nt,
                                        preferred_element_type=F32))
            s_bs.append(lax.dot_general(q_dec[CHUNK:], k_x, nt,
                                        preferred_element_type=F32))
            decay_cols.append(jnp.transpose(
                jnp.broadcast_to(jnp.exp(b_last_a + b_last_b), (LANES, dk))))
        for h in heads:
            vs = slice(h * dv, (h + 1) * dv)
            v = v_ref[rs, vs]
            o_inter = _dot(q_ints[h], state_ref[h].astype(BF16))
            kv = lax.dot_general(k_sts[h], v, tn, preferred_element_type=F32)
            p_a = jnp.where(mask_a, s_as[h], 0.0).astype(BF16)
            p_b = jnp.where(mask_b, s_bs[h], 0.0).astype(BF16)
            o = jnp.concatenate([_dot(p_a, v[:CHUNK]), _dot(p_b, v)], axis=0) + o_inter
            decay = jnp.concatenate([decay_cols[h]] * (dv // LANES), axis=1)
            state_ref[h] = state_ref[h] * decay + kv
            y = (o * lax.rsqrt(jnp.mean(o * o, axis=-1, keepdims=True) + EPS)).astype(BF16)
            o_ref[rs, vs] = y * ga_ref[rs, vs] + yb_ref[rs, vs]
        return carry

    lax.fori_loop(0, rows // PAIR, pair_body, 0, unroll=2)


def _gla_merge(qk, v, bcum, gate_a, y_b, batch, seq, rows=512):
    m, vd = v.shape
    kd = bcum.shape[1]
    dk = kd // GLA_HEADS
    dv = vd // GLA_HEADS
    nblk = seq // rows
    assert qk.shape[1] == 2 * kd
    row_map = lambda b, s: (b * nblk + s, 0)
    kern = functools.partial(_gla_kernel, rows=rows, dk=dk, dv=dv)
    return pl.pallas_call(
        kern,
        out_shape=jax.ShapeDtypeStruct((m, vd), BF16),
        grid=(batch, nblk),
        in_specs=[pl.BlockSpec((rows, kd), lambda b, s: (b * nblk + s, 0)),
                  pl.BlockSpec((rows, kd), lambda b, s: (b * nblk + s, 1)),
                  pl.BlockSpec((rows, vd), row_map),
                  pl.BlockSpec((rows, kd), row_map),
                  pl.BlockSpec((rows, vd), row_map),
                  pl.BlockSpec((rows, vd), row_map)],
        out_specs=pl.BlockSpec((rows, vd), row_map),
        scratch_shapes=[pltpu.VMEM((GLA_HEADS, dk, dv), F32)],
        compiler_params=_params(("arbitrary", "arbitrary")),
        name="gla_merge",
    )(qk, qk, v, bcum, gate_a, y_b)


def _out_proj_kernel(x_ref, m_ref, w_ref, o_ref):
    o_ref[...] = x_ref[...] + _dot(m_ref[...], w_ref[...])


def _out_proj(x, merged, w, tm=512):
    m, d = x.shape
    row = pl.BlockSpec((tm, d), lambda i: (i, 0))
    return pl.pallas_call(
        _out_proj_kernel,
        out_shape=jax.ShapeDtypeStruct((m, d), F32),
        grid=(m // tm,),
        in_specs=[row, row, pl.BlockSpec((d, d), lambda i: (0, 0))],
        out_specs=row,
        compiler_params=_params(("arbitrary",)),
        name="out_proj",
    )(x, merged, w)


def _ffn_kernel(h_ref, nw_ref, wg_ref, wu_ref, wd_ref, fw_ref, o_ref, hn_ref):
    f = pl.program_id(1)

    @pl.when(f == 0)
    def _():
        h = h_ref[...]
        y = h * lax.rsqrt(jnp.mean(h * h, axis=-1, keepdims=True) + EPS)
        hn_ref[...] = (y * nw_ref[...]).astype(BF16)
        o_ref[...] = h

    hn = hn_ref[...]
    gate = _dot(hn, wg_ref[...])
    up = _dot(hn, wu_ref[...])
    act = (gate * _sigmoid(gate) * up).astype(BF16)
    o_ref[...] += _dot(act, wd_ref[...])

    @pl.when(f == pl.num_programs(1) - 1)
    def _():
        z = o_ref[...]
        y = z * lax.rsqrt(jnp.mean(z * z, axis=-1, keepdims=True) + EPS)
        o_ref[...] = y * fw_ref[...]


def _ffn(h1, norm_w, w_gate_up, wd, final_w, tm=1024, tf=512):
    m, d = h1.shape
    hidden = wd.shape[0]
    nf = hidden // tf
    row = pl.BlockSpec((tm, d), lambda i, f: (i, 0))
    vec = pl.BlockSpec((1, d), lambda i, f: (0, 0))
    return pl.pallas_call(
        _ffn_kernel,
        out_shape=jax.ShapeDtypeStruct((m, d), F32),
        grid=(m // tm, nf),
        in_specs=[row, vec,
                  _col_blocks(d, tf, 0),
                  _col_blocks(d, tf, hidden),
                  pl.BlockSpec((tf, d), lambda i, f: (f, 0)),
                  vec],
        out_specs=row,
        scratch_shapes=[pltpu.VMEM((tm, d), BF16)],
        compiler_params=_params(("arbitrary", "arbitrary")),
        name="ffn",
    )(h1, norm_w.reshape(1, d), w_gate_up, w_gate_up, wd, final_w.reshape(1, d))


def _mixer_layer(x2, batch, seq, mix_norm_w, w_in, w_gk_up, b_gk_up, gla_norm_w,
                 conv_w, w_out, ffn_norm_w, w_gate_up, w_down, final_w):
    d = x2.shape[1]
    kd = w_gk_up.shape[1]
    vd = d

    o_v = 2 * kd
    o_go = o_v + vd
    o_gk = o_go + vd
    o_b = o_gk + GK_RANK
    o_c, o_x, o_ma, o_mb = o_b + d, o_b + 2 * d, o_b + 3 * d, o_b + 4 * d
    wt = jnp.swapaxes(w_in, 0, 1)
    w_gk = jnp.pad(wt[o_gk:o_b].astype(BF16), ((0, LANES - GK_RANK), (0, 0)))
    w_up = jnp.pad(w_gk_up.astype(BF16), ((0, LANES - GK_RANK), (0, 0)))
    conv_w8 = jnp.pad(conv_w, ((0, SUBLANES - CONV_K), (0, 0)))

    hn, bcum = _norm_decay(x2, mix_norm_w, w_gk, w_up, b_gk_up.reshape(1, kd))
    qk = _in_plain(hn, wt, 0, o_v, F32)
    v = _in_plain(hn, wt, o_v, vd, BF16)
    head_gain = jnp.tile(gla_norm_w, GLA_HEADS).reshape(1, vd)
    gate_a = _in_gate_a(hn, wt, o_go, o_ma, vd, head_gain)
    y_b = _in_conv(hn, wt, o_b, o_c, o_x, o_mb, d, conv_w8, seq)
    merged = _gla_merge(qk, v, bcum, gate_a, y_b, batch, seq)
    h1 = _out_proj(x2, merged, w_out.astype(BF16))
    return _ffn(h1, ffn_norm_w, w_gate_up.astype(BF16), w_down.astype(BF16),
                final_w)


def kernel(x, mix_norm_w, w_in, w_gk_up, b_gk_up, gla_norm_w, conv_w, w_out,
           ffn_norm_w, w_gate_up, w_down, final_norm_w):
    batch, seq, d = x.shape
    assert mix_norm_w.shape[0] == 1, "single-layer block"
    out = _mixer_layer(x.reshape(batch * seq, d), batch, seq, mix_norm_w[0],
                       w_in[0], w_gk_up[0], b_gk_up[0], gla_norm_w[0], conv_w[0],
                       w_out[0], ffn_norm_w[0], w_gate_up[0], w_down[0],
                       final_norm_w)
    return out.reshape(batch, seq, d)
```

```python
import functools

import jax
import jax.numpy as jnp
from jax import lax
from jax.experimental import pallas as pl
from jax.experimental.pallas import tpu as pltpu

EPS = 1e-6
GLA_HEADS = 4
GK_RANK = 16
GATE_LOGIT_NORMALIZER = 16.0
CHUNK = 64
PAIR = 2 * CHUNK
CUM_ROWS = PAIR
CONV_K = 3
ROW_SUB = 256
ROW_SUB_LAST = 128

LANES = 128
SUBLANES = 8
VMEM_LIMIT = 56 * 1024 * 1024
VMEM_LIMIT_FFN = 60 * 1024 * 1024

F32 = jnp.float32
BF16 = jnp.bfloat16


def _params(semantics, vmem_limit=VMEM_LIMIT):
    return pltpu.CompilerParams(dimension_semantics=semantics,
                                vmem_limit_bytes=vmem_limit)


def _sigmoid(x):
    return 1.0 / (1.0 + jnp.exp(-x))


def _dot(a, b):
    return jnp.dot(a, b, preferred_element_type=F32)


def _norm_decay_kernel(x_ref, nw_ref, wgk_ref, wup_ref, b_ref, hn_ref, o_ref, *, tm):
    row = lax.broadcasted_iota(jnp.int32, (CUM_ROWS, CUM_ROWS), 0)
    col = lax.broadcasted_iota(jnp.int32, (CUM_ROWS, CUM_ROWS), 1)
    tri = ((row >= col) & (row // CHUNK == col // CHUNK)).astype(BF16)
    for r in range(0, tm, CUM_ROWS):
        rows = slice(r, r + CUM_ROWS)
        x = x_ref[rows, :]
        y = x * lax.rsqrt(jnp.mean(x * x, axis=-1, keepdims=True) + EPS)
        hn = (y * nw_ref[...]).astype(BF16)
        hn_ref[rows, :] = hn
        low = lax.dot_general(hn, wgk_ref[...], (((1,), (1,)), ((), ())),
                              preferred_element_type=F32).astype(BF16)
        gk = _dot(low, wup_ref[...]) + b_ref[...]
        log_sig = jnp.minimum(gk, 0.0) - jnp.log(1.0 + jnp.exp(-jnp.abs(gk)))
        g = log_sig * (1.0 / GATE_LOGIT_NORMALIZER)
        g1 = g.astype(BF16)
        rem = g - g1.astype(F32)
        g2 = rem.astype(BF16)
        g3 = (rem - g2.astype(F32)).astype(BF16)
        o_ref[rows, :] = _dot(tri, g1) + _dot(tri, g2) + _dot(tri, g3)


def _norm_decay(x, norm_w, wgk, wup, b, tm=512):
    m, d = x.shape
    r = wgk.shape[0]
    n = wup.shape[1]
    return pl.pallas_call(
        functools.partial(_norm_decay_kernel, tm=tm),
        out_shape=(jax.ShapeDtypeStruct((m, d), BF16),
                   jax.ShapeDtypeStruct((m, n), F32)),
        grid=(m // tm,),
        in_specs=[pl.BlockSpec((tm, d), lambda i: (i, 0)),
                  pl.BlockSpec((1, d), lambda i: (0, 0)),
                  pl.BlockSpec((r, d), lambda i: (0, 0)),
                  pl.BlockSpec((r, n), lambda i: (0, 0)),
                  pl.BlockSpec((1, n), lambda i: (0, 0))],
        out_specs=(pl.BlockSpec((tm, d), lambda i: (i, 0)),
                   pl.BlockSpec((tm, n), lambda i: (i, 0))),
        compiler_params=_params(("arbitrary",)),
        name="norm_decay",
    )(x, norm_w.reshape(1, d), wgk, wup, b)


def _weight_spec(d, tn, col0):
    assert col0 % SUBLANES == 0
    return pl.BlockSpec((pl.Element(tn), pl.Element(d)),
                        lambda j, i: (pl.multiple_of(col0 + j * tn, SUBLANES), 0))


def _load_weight(wt_ref):
    return wt_ref[...].T.astype(BF16)


def _row_subtiles(tm):
    r = 0
    while tm - r > ROW_SUB:
        yield r, ROW_SUB
        r += ROW_SUB
    while r < tm:
        yield r, ROW_SUB_LAST
        r += ROW_SUB_LAST


def _hn_spec(tm, d):
    return pl.BlockSpec((tm, d), lambda j, i: (i, 0))


def _out_spec(tm, tn):
    return pl.BlockSpec((tm, tn), lambda j, i: (i, j))


def _plain_kernel(h_ref, wt_ref, o_ref, wbf_ref):
    @pl.when(pl.program_id(1) == 0)
    def _():
        wbf_ref[...] = _load_weight(wt_ref)

    o_ref[...] = _dot(h_ref[...], wbf_ref[...]).astype(o_ref.dtype)


def _in_plain(hn, wt, col0, n, out_dtype, tm=1024, tn=1024):
    m, d = hn.shape
    return pl.pallas_call(
        _plain_kernel,
        out_shape=jax.ShapeDtypeStruct((m, n), out_dtype),
        grid=(n // tn, m // tm),
        in_specs=[_hn_spec(tm, d), _weight_spec(d, tn, col0)],
        out_specs=_out_spec(tm, tn),
        scratch_shapes=[pltpu.VMEM((d, tn), BF16)],
        compiler_params=_params(("arbitrary", "arbitrary")),
        name="in_plain",
    )(hn, wt)


def _gate_a_kernel(h_ref, gt_ref, mt_ref, o_ref, wg_ref, wm_ref, *, tm):
    @pl.when(pl.program_id(1) == 0)
    def _():
        wg_ref[...] = _load_weight(gt_ref)
        wm_ref[...] = _load_weight(mt_ref)

    for r, n in _row_subtiles(tm):
        h = h_ref[r:r + n, :]
        g = _dot(h, wg_ref[...])
        m = _dot(h, wm_ref[...])
        o_ref[r:r + n, :] = (g * _sigmoid(g) * _sigmoid(m)).astype(o_ref.dtype)


def _in_gate_a(hn, wt, g0, m0, n, tm=2048, tn=512):
    m, d = hn.shape
    return pl.pallas_call(
        functools.partial(_gate_a_kernel, tm=tm),
        out_shape=jax.ShapeDtypeStruct((m, n), BF16),
        grid=(n // tn, m // tm),
        in_specs=[_hn_spec(tm, d), _weight_spec(d, tn, g0), _weight_spec(d, tn, m0)],
        out_specs=_out_spec(tm, tn),
        scratch_shapes=[pltpu.VMEM((d, tn), BF16)] * 2,
        compiler_params=_params(("arbitrary", "arbitrary")),
        name="in_gate_a",
    )(hn, wt, wt)


def _conv_kernel(h_ref, bt_ref, ct_ref, xt_ref, mt_ref, cw_ref, o_ref,
                 wb_ref, wc_ref, wx_ref, wm_ref, work_ref, halo_ref,
                 *, tm, tiles_per_seq):
    i = pl.program_id(1)

    @pl.when(i == 0)
    def _():
        wb_ref[...] = _load_weight(bt_ref)
        wc_ref[...] = _load_weight(ct_ref)
        wx_ref[...] = _load_weight(xt_ref)
        wm_ref[...] = _load_weight(mt_ref)

    @pl.when(i % tiles_per_seq == 0)
    def _():
        work_ref[0:SUBLANES, :] = jnp.zeros((SUBLANES, work_ref.shape[1]), F32)

    @pl.when(i % tiles_per_seq != 0)
    def _():
        work_ref[0:SUBLANES, :] = halo_ref[...]

    cw = cw_ref[...]
    for r, n in _row_subtiles(tm):
        h = h_ref[r:r + n, :]
        u = _dot(h, wc_ref[...]) * _dot(h, wx_ref[...])
        work_ref[SUBLANES + r:SUBLANES + r + n, :] = u
        u1 = work_ref[SUBLANES - 1 + r:SUBLANES - 1 + r + n, :]
        u2 = work_ref[SUBLANES - 2 + r:SUBLANES - 2 + r + n, :]
        conv = cw[0:1, :] * u2 + cw[1:2, :] * u1 + cw[2:3, :] * u
        y_b = _dot(h, wb_ref[...]) * conv
        o_ref[r:r + n, :] = (
            _sigmoid(_dot(h, wm_ref[...])) * y_b).astype(o_ref.dtype)
    halo_ref[...] = work_ref[tm:tm + SUBLANES, :]


def _in_conv(hn, wt, b0, c0, x0, m0, n, conv_w, seq, tm=2048, tn=256):
    m, d = hn.shape
    kern = functools.partial(_conv_kernel, tm=tm, tiles_per_seq=seq // tm)
    return pl.pallas_call(
        kern,
        out_shape=jax.ShapeDtypeStruct((m, n), BF16),
        grid=(n // tn, m // tm),
        in_specs=[_hn_spec(tm, d)]
                 + [_weight_spec(d, tn, c) for c in (b0, c0, x0, m0)]
                 + [pl.BlockSpec((SUBLANES, tn), lambda j, i: (0, j))],
        out_specs=_out_spec(tm, tn),
        scratch_shapes=[pltpu.VMEM((d, tn), BF16)] * 4
                       + [pltpu.VMEM((tm + SUBLANES, tn), F32),
                          pltpu.VMEM((SUBLANES, tn), F32)],
        compiler_params=_params(("arbitrary", "arbitrary")),
        name="in_conv",
    )(hn, wt, wt, wt, wt, conv_w)


def _gla_kernel(q_ref, k_ref, v_ref, b_ref, ga_ref, yb_ref, nw_ref, o_ref,
                state_ref, *, rows, dk, dv):
    @pl.when(pl.program_id(1) == 0)
    def _():
        state_ref[...] = jnp.zeros(state_ref.shape, F32)

    row = lax.broadcasted_iota(jnp.int32, (CHUNK, PAIR), 0)
    col = lax.broadcasted_iota(jnp.int32, (CHUNK, PAIR), 1)
    mask_a = (row >= col)[:, :CHUNK]
    mask_b = col <= row + CHUNK
    scale = dk ** -0.5
    nw = nw_ref[...]
    nt = (((1,), (1,)), ((), ()))
    tn = (((0,), (0,)), ((), ()))

    def pair_body(c, carry):
        r0 = pl.multiple_of(c * PAIR, PAIR)
        rs = pl.ds(r0, PAIR)
        heads = range(GLA_HEADS)
        s_as, s_bs, q_ints, k_sts, decay_cols = [], [], [], [], []
        for h in heads:
            ks = slice(h * dk, (h + 1) * dk)
            b = b_ref[rs, ks]
            b_last_a = b[CHUNK - 1:CHUNK, :]
            b_last_b = b[PAIR - 1:PAIR, :]
            b_end = jnp.concatenate([jnp.broadcast_to(b_last_a, (CHUNK, dk)),
                                     jnp.broadcast_to(b_last_b, (CHUNK, dk))], axis=0)
            q = q_ref[rs, ks]
            k = k_ref[rs, ks]
            qd = q * (scale * jnp.exp(b))
            ke = k * jnp.exp(b_end - b)
            q_dec = qd.astype(BF16)
            k_dec = (k * jnp.exp(-b)).astype(BF16)
            k_end = ke.astype(BF16)
            q_ints.append(jnp.concatenate(
                [q_dec[:CHUNK], (qd[CHUNK:] * jnp.exp(b_last_a)).astype(BF16)], axis=0))
            k_sts.append(jnp.concatenate(
                [(ke[:CHUNK] * jnp.exp(b_last_b)).astype(BF16), k_end[CHUNK:]], axis=0))
            k_x = jnp.concatenate([k_end[:CHUNK], k_dec[CHUNK:]], axis=0)
            s_as.append(lax.dot_general(q_dec[:CHUNK], k_dec[:CHUNK], nt,
                                        preferred_element_type=F32))
            s_bs.append(lax.dot_general(q_dec[CHUNK:], k_x, nt,
                                        preferred_element_type=F32))
            decay_cols.append(jnp.transpose(
                jnp.broadcast_to(jnp.exp(b_last_a + b_last_b), (LANES, dk))))
        for h in heads:
            vs = slice(h * dv, (h + 1) * dv)
            v = v_ref[rs, vs]
            o_inter = _dot(q_ints[h], state_ref[h].astype(BF16))
            kv = lax.dot_general(k_sts[h], v, tn, preferred_element_type=F32)
            p_a = jnp.where(mask_a, s_as[h], 0.0).astype(BF16)
            p_b = jnp.where(mask_b, s_bs[h], 0.0).astype(BF16)
            o = jnp.concatenate([_dot(p_a, v[:CHUNK]), _dot(p_b, v)], axis=0) + o_inter
            decay = jnp.concatenate([decay_cols[h]] * (dv // LANES), axis=1)
            state_ref[h] = state_ref[h] * decay + kv
            y = o * lax.rsqrt(jnp.mean(o * o, axis=-1, keepdims=True) + EPS) * nw
            y = y.astype(BF16)
            o_ref[rs, vs] = y * ga_ref[rs, vs] + yb_ref[rs, vs]
        return carry

    lax.fori_loop(0, rows // PAIR, pair_body, 0, unroll=2)


def _gla_merge(qk, v, bcum, gate_a, y_b, norm_w, batch, seq, rows=512):
    m, vd = v.shape
    kd = bcum.shape[1]
    dk = kd // GLA_HEADS
    dv = vd // GLA_HEADS
    nblk = seq // rows
    assert qk.shape[1] == 2 * kd
    row_map = lambda b, s: (b * nblk + s, 0)
    kern = functools.partial(_gla_kernel, rows=rows, dk=dk, dv=dv)
    return pl.pallas_call(
        kern,
        out_shape=jax.ShapeDtypeStruct((m, vd), BF16),
        grid=(batch, nblk),
        in_specs=[pl.BlockSpec((rows, kd), lambda b, s: (b * nblk + s, 0)),
                  pl.BlockSpec((rows, kd), lambda b, s: (b * nblk + s, 1)),
                  pl.BlockSpec((rows, vd), row_map),
                  pl.BlockSpec((rows, kd), row_map),
                  pl.BlockSpec((rows, vd), row_map),
                  pl.BlockSpec((rows, vd), row_map),
                  pl.BlockSpec((1, dv), lambda b, s: (0, 0))],
        out_specs=pl.BlockSpec((rows, vd), row_map),
        scratch_shapes=[pltpu.VMEM((GLA_HEADS, dk, dv), F32)],
        compiler_params=_params(("arbitrary", "arbitrary")),
        name="gla_merge",
    )(qk, qk, v, bcum, gate_a, y_b, norm_w)


def _out_proj_kernel(x_ref, m_ref, w_ref, o_ref):
    o_ref[...] = x_ref[...] + _dot(m_ref[...], w_ref[...])


def _out_proj(x, merged, w, tm=512):
    m, d = x.shape
    row = pl.BlockSpec((tm, d), lambda i: (i, 0))
    return pl.pallas_call(
        _out_proj_kernel,
        out_shape=jax.ShapeDtypeStruct((m, d), F32),
        grid=(m // tm,),
        in_specs=[row, row, pl.BlockSpec((d, d), lambda i: (0, 0))],
        out_specs=row,
        compiler_params=_params(("arbitrary",)),
        name="out_proj",
    )(x, merged, w)


def _ffn_kernel(h_ref, nw_ref, wg_ref, wu_ref, wd_ref, fw_ref, o_ref,
                hn_ref, act_ref):
    f = pl.program_id(1)
    nf = pl.num_programs(1) - 1

    def activation(slot):
        hn = hn_ref[...]
        gate = _dot(hn, wg_ref[...])
        up = _dot(hn, wu_ref[...])
        act_ref[slot] = (gate * _sigmoid(gate) * up).astype(BF16)

    @pl.when(f == 0)
    def _():
        h = h_ref[...]
        y = h * lax.rsqrt(jnp.mean(h * h, axis=-1, keepdims=True) + EPS)
        hn_ref[...] = (y * nw_ref[...]).astype(BF16)
        o_ref[...] = h
        activation(0)

    @pl.when((f > 0) & (f < nf))
    def _():
        activation(f % 2)
        o_ref[...] += _dot(act_ref[(f - 1) % 2], wd_ref[...])

    @pl.when(f == nf)
    def _():
        z = o_ref[...] + _dot(act_ref[(f - 1) % 2], wd_ref[...])
        y = z * lax.rsqrt(jnp.mean(z * z, axis=-1, keepdims=True) + EPS)
        o_ref[...] = y * fw_ref[...]


def _ffn(h1, norm_w, w_gate_up, wd, final_w, tm=1024, tf=512):
    m, d = h1.shape
    hidden = wd.shape[0]
    nf = hidden // tf
    up0 = hidden // tf
    row = pl.BlockSpec((tm, d), lambda i, f: (i, 0))
    vec = pl.BlockSpec((1, d), lambda i, f: (0, 0))
    return pl.pallas_call(
        _ffn_kernel,
        out_shape=jax.ShapeDtypeStruct((m, d), F32),
        grid=(m // tm, nf + 1),
        in_specs=[row, vec,
                  pl.BlockSpec((d, tf), lambda i, f: (0, jnp.minimum(f, nf - 1))),
                  pl.BlockSpec((d, tf), lambda i, f: (0, up0 + jnp.minimum(f, nf - 1))),
                  pl.BlockSpec((tf, d), lambda i, f: (jnp.maximum(f - 1, 0), 0)),
                  vec],
        out_specs=row,
        scratch_shapes=[pltpu.VMEM((tm, d), BF16), pltpu.VMEM((2, tm, tf), BF16)],
        compiler_params=_params(("arbitrary", "arbitrary"), VMEM_LIMIT_FFN),
        name="ffn",
    )(h1, norm_w.reshape(1, d), w_gate_up, w_gate_up, wd, final_w.reshape(1, d))


def _mixer_layer(x2, batch, seq, mix_norm_w, w_in, w_gk_up, b_gk_up, gla_norm_w,
                 conv_w, w_out, ffn_norm_w, w_gate_up, w_down, final_w):
    d = x2.shape[1]
    kd = w_gk_up.shape[1]
    vd = d

    o_v = 2 * kd
    o_go = o_v + vd
    o_gk = o_go + vd
    o_b = o_gk + GK_RANK
    o_c, o_x, o_ma, o_mb = o_b + d, o_b + 2 * d, o_b + 3 * d, o_b + 4 * d
    wt = jnp.swapaxes(w_in, 0, 1)
    w_gk = jnp.pad(wt[o_gk:o_b].astype(BF16), ((0, LANES - GK_RANK), (0, 0)))
    w_up = jnp.pad(w_gk_up.astype(BF16), ((0, LANES - GK_RANK), (0, 0)))
    conv_w8 = jnp.pad(conv_w, ((0, SUBLANES - CONV_K), (0, 0)))

    hn, bcum = _norm_decay(x2, mix_norm_w, w_gk, w_up, b_gk_up.reshape(1, kd))
    qk = _in_plain(hn, wt, 0, o_v, F32)
    v = _in_plain(hn, wt, o_v, vd, BF16)
    gate_a = _in_gate_a(hn, wt, o_go, o_ma, vd)
    y_b = _in_conv(hn, wt, o_b, o_c, o_x, o_mb, d, conv_w8, seq)
    merged = _gla_merge(qk, v, bcum, gate_a, y_b, gla_norm_w.reshape(1, -1),
                        batch, seq)
    h1 = _out_proj(x2, merged, w_out.astype(BF16))
    return _ffn(h1, ffn_norm_w, w_gate_up.astype(BF16), w_down.astype(BF16),
                final_w)


def kernel(x, mix_norm_w, w_in, w_gk_up, b_gk_up, gla_norm_w, conv_w, w_out,
           ffn_norm_w, w_gate_up, w_down, final_norm_w):
    batch, seq, d = x.shape
    assert mix_norm_w.shape[0] == 1, "single-layer block"
    out = _mixer_layer(x.reshape(batch * seq, d), batch, seq, mix_norm_w[0],
                       w_in[0], w_gk_up[0], b_gk_up[0], gla_norm_w[0], conv_w[0],
                       w_out[0], ffn_norm_w[0], w_gate_up[0], w_down[0],
                       final_norm_w)
    return out.reshape(batch, seq, d)
```

```python
import functools

import jax
import jax.numpy as jnp
from jax import lax
from jax.experimental import pallas as pl
from jax.experimental.pallas import tpu as pltpu

EPS = 1e-6
GLA_HEADS = 4
GK_RANK = 16
GATE_LOGIT_NORMALIZER = 16.0
CHUNK = 64
PAIR = 2 * CHUNK
CUM_ROWS = PAIR
CONV_K = 3
ROW_SUB = 256
ROW_SUB_LAST = 128

LANES = 128
SUBLANES = 8
VMEM_LIMIT = 56 * 1024 * 1024

F32 = jnp.float32
BF16 = jnp.bfloat16


def _params(semantics):
    return pltpu.CompilerParams(dimension_semantics=semantics,
                                vmem_limit_bytes=VMEM_LIMIT)


def _sigmoid(x):
    return 1.0 / (1.0 + jnp.exp(-x))


def _dot(a, b):
    return jnp.dot(a, b, preferred_element_type=F32)


def _norm_decay_kernel(x_ref, nw_ref, wgk_ref, wup_ref, b_ref, hn_ref, o_ref, *, tm):
    row = lax.broadcasted_iota(jnp.int32, (CUM_ROWS, CUM_ROWS), 0)
    col = lax.broadcasted_iota(jnp.int32, (CUM_ROWS, CUM_ROWS), 1)
    tri = ((row >= col) & (row // CHUNK == col // CHUNK)).astype(BF16)
    lows = []
    for r in range(0, tm, CUM_ROWS):
        rows = slice(r, r + CUM_ROWS)
        x = x_ref[rows, :]
        y = x * lax.rsqrt(jnp.mean(x * x, axis=-1, keepdims=True) + EPS)
        hn = (y * nw_ref[...]).astype(BF16)
        hn_ref[rows, :] = hn
        lows.append(lax.dot_general(hn, wgk_ref[...], (((1,), (1,)), ((), ())),
                                    preferred_element_type=F32).astype(BF16))
    gks = [_dot(low, wup_ref[...]) + b_ref[...] for low in lows]
    for r, gk in zip(range(0, tm, CUM_ROWS), gks):
        rows = slice(r, r + CUM_ROWS)
        log_sig = jnp.minimum(gk, 0.0) - jnp.log(1.0 + jnp.exp(-jnp.abs(gk)))
        g = log_sig * (1.0 / GATE_LOGIT_NORMALIZER)
        g1 = g.astype(BF16)
        rem = g - g1.astype(F32)
        g2 = rem.astype(BF16)
        g3 = (rem - g2.astype(F32)).astype(BF16)
        o_ref[rows, :] = _dot(tri, g1) + _dot(tri, g2) + _dot(tri, g3)


def _norm_decay(x, norm_w, wgk, wup, b, tm=512):
    m, d = x.shape
    r = wgk.shape[0]
    n = wup.shape[1]
    return pl.pallas_call(
        functools.partial(_norm_decay_kernel, tm=tm),
        out_shape=(jax.ShapeDtypeStruct((m, d), BF16),
                   jax.ShapeDtypeStruct((m, n), F32)),
        grid=(m // tm,),
        in_specs=[pl.BlockSpec((tm, d), lambda i: (i, 0)),
                  pl.BlockSpec((1, d), lambda i: (0, 0)),
                  pl.BlockSpec((r, d), lambda i: (0, 0)),
                  pl.BlockSpec((r, n), lambda i: (0, 0)),
                  pl.BlockSpec((1, n), lambda i: (0, 0))],
        out_specs=(pl.BlockSpec((tm, d), lambda i: (i, 0)),
                   pl.BlockSpec((tm, n), lambda i: (i, 0))),
        compiler_params=_params(("arbitrary",)),
        name="norm_decay",
    )(x, norm_w.reshape(1, d), wgk, wup, b)


def _weight_spec(d, tn, col0):
    assert col0 % SUBLANES == 0
    return pl.BlockSpec((pl.Element(tn), pl.Element(d)),
                        lambda j, i: (pl.multiple_of(col0 + j * tn, SUBLANES), 0))


def _load_weight(wt_ref):
    return wt_ref[...].T.astype(BF16)


def _row_subtiles(tm):
    r = 0
    while tm - r > ROW_SUB:
        yield r, ROW_SUB
        r += ROW_SUB
    while r < tm:
        yield r, ROW_SUB_LAST
        r += ROW_SUB_LAST


def _hn_spec(tm, d):
    return pl.BlockSpec((tm, d), lambda j, i: (i, 0))


def _out_spec(tm, tn):
    return pl.BlockSpec((tm, tn), lambda j, i: (i, j))


def _plain_kernel(h_ref, wt_ref, o_ref, wbf_ref):
    @pl.when(pl.program_id(1) == 0)
    def _():
        wbf_ref[...] = _load_weight(wt_ref)

    o_ref[...] = _dot(h_ref[...], wbf_ref[...]).astype(o_ref.dtype)


def _in_plain(hn, wt, col0, n, out_dtype, tm=1024, tn=1024):
    m, d = hn.shape
    return pl.pallas_call(
        _plain_kernel,
        out_shape=jax.ShapeDtypeStruct((m, n), out_dtype),
        grid=(n // tn, m // tm),
        in_specs=[_hn_spec(tm, d), _weight_spec(d, tn, col0)],
        out_specs=_out_spec(tm, tn),
        scratch_shapes=[pltpu.VMEM((d, tn), BF16)],
        compiler_params=_params(("arbitrary", "arbitrary")),
        name="in_plain",
    )(hn, wt)


def _gate_a_kernel(h_ref, gt_ref, mt_ref, o_ref, wg_ref, wm_ref, *, tm):
    @pl.when(pl.program_id(1) == 0)
    def _():
        wg_ref[...] = _load_weight(gt_ref)
        wm_ref[...] = _load_weight(mt_ref)

    for r, n in _row_subtiles(tm):
        h = h_ref[r:r + n, :]
        g = _dot(h, wg_ref[...])
        m = _dot(h, wm_ref[...])
        o_ref[r:r + n, :] = (g * _sigmoid(g) * _sigmoid(m)).astype(o_ref.dtype)


def _in_gate_a(hn, wt, g0, m0, n, tm=2048, tn=512):
    m, d = hn.shape
    return pl.pallas_call(
        functools.partial(_gate_a_kernel, tm=tm),
        out_shape=jax.ShapeDtypeStruct((m, n), BF16),
        grid=(n // tn, m // tm),
        in_specs=[_hn_spec(tm, d), _weight_spec(d, tn, g0), _weight_spec(d, tn, m0)],
        out_specs=_out_spec(tm, tn),
        scratch_shapes=[pltpu.VMEM((d, tn), BF16)] * 2,
        compiler_params=_params(("arbitrary", "arbitrary")),
        name="in_gate_a",
    )(hn, wt, wt)


def _conv_kernel(h_ref, bt_ref, ct_ref, xt_ref, mt_ref, cw_ref, o_ref,
                 wb_ref, wc_ref, wx_ref, wm_ref, work_ref, halo_ref,
                 *, tm, tiles_per_seq):
    i = pl.program_id(1)

    @pl.when(i == 0)
    def _():
        wb_ref[...] = _load_weight(bt_ref)
        wc_ref[...] = _load_weight(ct_ref)
        wx_ref[...] = _load_weight(xt_ref)
        wm_ref[...] = _load_weight(mt_ref)

    @pl.when(i % tiles_per_seq == 0)
    def _():
        work_ref[0:SUBLANES, :] = jnp.zeros((SUBLANES, work_ref.shape[1]), F32)

    @pl.when(i % tiles_per_seq != 0)
    def _():
        work_ref[0:SUBLANES, :] = halo_ref[...]

    cw = cw_ref[...]
    for r, n in _row_subtiles(tm):
        h = h_ref[r:r + n, :]
        u = _dot(h, wc_ref[...]) * _dot(h, wx_ref[...])
        work_ref[SUBLANES + r:SUBLANES + r + n, :] = u
        u1 = work_ref[SUBLANES - 1 + r:SUBLANES - 1 + r + n, :]
        u2 = work_ref[SUBLANES - 2 + r:SUBLANES - 2 + r + n, :]
        conv = cw[0:1, :] * u2 + cw[1:2, :] * u1 + cw[2:3, :] * u
        y_b = _dot(h, wb_ref[...]) * conv
        o_ref[r:r + n, :] = (
            _sigmoid(_dot(h, wm_ref[...])) * y_b).astype(o_ref.dtype)
    halo_ref[...] = work_ref[tm:tm + SUBLANES, :]


def _in_conv(hn, wt, b0, c0, x0, m0, n, conv_w, seq, tm=2048, tn=256):
    m, d = hn.shape
    kern = functools.partial(_conv_kernel, tm=tm, tiles_per_seq=seq // tm)
    return pl.pallas_call(
        kern,
        out_shape=jax.ShapeDtypeStruct((m, n), BF16),
        grid=(n // tn, m // tm),
        in_specs=[_hn_spec(tm, d)]
                 + [_weight_spec(d, tn, c) for c in (b0, c0, x0, m0)]
                 + [pl.BlockSpec((SUBLANES, tn), lambda j, i: (0, j))],
        out_specs=_out_spec(tm, tn),
        scratch_shapes=[pltpu.VMEM((d, tn), BF16)] * 4
                       + [pltpu.VMEM((tm + SUBLANES, tn), F32),
                          pltpu.VMEM((SUBLANES, tn), F32)],
        compiler_params=_params(("arbitrary", "arbitrary")),
        name="in_conv",
    )(hn, wt, wt, wt, wt, conv_w)


def _gla_kernel(q_ref, k_ref, v_ref, b_ref, ga_ref, yb_ref, nw_ref, o_ref,
                state_ref, *, rows, dk, dv):
    @pl.when(pl.program_id(1) == 0)
    def _():
        state_ref[...] = jnp.zeros(state_ref.shape, F32)

    row = lax.broadcasted_iota(jnp.int32, (CHUNK, PAIR), 0)
    col = lax.broadcasted_iota(jnp.int32, (CHUNK, PAIR), 1)
    mask_a = (row >= col)[:, :CHUNK]
    mask_b = col <= row + CHUNK
    scale = dk ** -0.5
    nw = nw_ref[...]
    nt = (((1,), (1,)), ((), ()))
    tn = (((0,), (0,)), ((), ()))

    def pair_body(c, carry):
        r0 = pl.multiple_of(c * PAIR, PAIR)
        rs = pl.ds(r0, PAIR)
        heads = range(GLA_HEADS)
        s_as, s_bs, q_ints, k_sts, decay_cols = [], [], [], [], []
        for h in heads:
            ks = slice(h * dk, (h + 1) * dk)
            b = b_ref[rs, ks]
            b_last_a = b[CHUNK - 1:CHUNK, :]
            b_last_b = b[PAIR - 1:PAIR, :]
            b_end = jnp.concatenate([jnp.broadcast_to(b_last_a, (CHUNK, dk)),
                                     jnp.broadcast_to(b_last_b, (CHUNK, dk))], axis=0)
            q = q_ref[rs, ks]
            k = k_ref[rs, ks]
            qd = q * (scale * jnp.exp(b))
            ke = k * jnp.exp(b_end - b)
            q_dec = qd.astype(BF16)
            k_dec = (k * jnp.exp(-b)).astype(BF16)
            k_end = ke.astype(BF16)
            q_ints.append(jnp.concatenate(
                [q_dec[:CHUNK], (qd[CHUNK:] * jnp.exp(b_last_a)).astype(BF16)], axis=0))
            k_sts.append(jnp.concatenate(
                [(ke[:CHUNK] * jnp.exp(b_last_b)).astype(BF16), k_end[CHUNK:]], axis=0))
            k_x = jnp.concatenate([k_end[:CHUNK], k_dec[CHUNK:]], axis=0)
            s_as.append(lax.dot_general(q_dec[:CHUNK], k_dec[:CHUNK], nt,
                                        preferred_element_type=F32))
            s_bs.append(lax.dot_general(q_dec[CHUNK:], k_x, nt,
                                        preferred_element_type=F32))
            decay_cols.append(jnp.transpose(
                jnp.broadcast_to(jnp.exp(b_last_a + b_last_b), (LANES, dk))))
        for h in heads:
            vs = slice(h * dv, (h + 1) * dv)
            v = v_ref[rs, vs]
            o_inter = _dot(q_ints[h], state_ref[h].astype(BF16))
            kv = lax.dot_general(k_sts[h], v, tn, preferred_element_type=F32)
            p_a = jnp.where(mask_a, s_as[h], 0.0).astype(BF16)
            p_b = jnp.where(mask_b, s_bs[h], 0.0).astype(BF16)
            o = jnp.concatenate([_dot(p_a, v[:CHUNK]), _dot(p_b, v)], axis=0) + o_inter
            decay = jnp.concatenate([decay_cols[h]] * (dv // LANES), axis=1)
            state_ref[h] = state_ref[h] * decay + kv
            y = o * lax.rsqrt(jnp.mean(o * o, axis=-1, keepdims=True) + EPS) * nw
            y = y.astype(BF16)
            o_ref[rs, vs] = y * ga_ref[rs, vs] + yb_ref[rs, vs]
        return carry

    lax.fori_loop(0, rows // PAIR, pair_body, 0, unroll=2)


def _gla_merge(qk, v, bcum, gate_a, y_b, norm_w, batch, seq, rows=512):
    m, vd = v.shape
    kd = bcum.shape[1]
    dk = kd // GLA_HEADS
    dv = vd // GLA_HEADS
    nblk = seq // rows
    assert qk.shape[1] == 2 * kd
    row_map = lambda b, s: (b * nblk + s, 0)
    kern = functools.partial(_gla_kernel, rows=rows, dk=dk, dv=dv)
    return pl.pallas_call(
        kern,
        out_shape=jax.ShapeDtypeStruct((m, vd), BF16),
        grid=(batch, nblk),
        in_specs=[pl.BlockSpec((rows, kd), lambda b, s: (b * nblk + s, 0)),
                  pl.BlockSpec((rows, kd), lambda b, s: (b * nblk + s, 1)),
                  pl.BlockSpec((rows, vd), row_map),
                  pl.BlockSpec((rows, kd), row_map),
                  pl.BlockSpec((rows, vd), row_map),
                  pl.BlockSpec((rows, vd), row_map),
                  pl.BlockSpec((1, dv), lambda b, s: (0, 0))],
        out_specs=pl.BlockSpec((rows, vd), row_map),
        scratch_shapes=[pltpu.VMEM((GLA_HEADS, dk, dv), F32)],
        compiler_params=_params(("arbitrary", "arbitrary")),
        name="gla_merge",
    )(qk, qk, v, bcum, gate_a, y_b, norm_w)


def _out_proj_kernel(x_ref, m_ref, w_ref, o_ref):
    o_ref[...] = x_ref[...] + _dot(m_ref[...], w_ref[...])


def _out_proj(x, merged, w, tm=512):
    m, d = x.shape
    row = pl.BlockSpec((tm, d), lambda i: (i, 0))
    return pl.pallas_call(
        _out_proj_kernel,
        out_shape=jax.ShapeDtypeStruct((m, d), F32),
        grid=(m // tm,),
        in_specs=[row, row, pl.BlockSpec((d, d), lambda i: (0, 0))],
        out_specs=row,
        compiler_params=_params(("arbitrary",)),
        name="out_proj",
    )(x, merged, w)


def _ffn_kernel(h_ref, nw_ref, wg_ref, wu_ref, wd_ref, fw_ref, o_ref, hn_ref):
    f = pl.program_id(1)

    @pl.when(f == 0)
    def _():
        h = h_ref[...]
        y = h * lax.rsqrt(jnp.mean(h * h, axis=-1, keepdims=True) + EPS)
        hn_ref[...] = (y * nw_ref[...]).astype(BF16)
        o_ref[...] = h

    hn = hn_ref[...]
    gate = _dot(hn, wg_ref[...])
    up = _dot(hn, wu_ref[...])
    act = (gate * _sigmoid(gate) * up).astype(BF16)
    o_ref[...] += _dot(act, wd_ref[...])

    @pl.when(f == pl.num_programs(1) - 1)
    def _():
        z = o_ref[...]
        y = z * lax.rsqrt(jnp.mean(z * z, axis=-1, keepdims=True) + EPS)
        o_ref[...] = y * fw_ref[...]


def _ffn(h1, norm_w, w_gate_up, wd, final_w, tm=1024, tf=512):
    m, d = h1.shape
    hidden = wd.shape[0]
    nf = hidden // tf
    row = pl.BlockSpec((tm, d), lambda i, f: (i, 0))
    vec = pl.BlockSpec((1, d), lambda i, f: (0, 0))
    return pl.pallas_call(
        _ffn_kernel,
        out_shape=jax.ShapeDtypeStruct((m, d), F32),
        grid=(m // tm, nf),
        in_specs=[row, vec,
                  pl.BlockSpec((d, tf), lambda i, f: (0, f)),
                  pl.BlockSpec((d, tf), lambda i, f: (0, nf + f)),
                  pl.BlockSpec((tf, d), lambda i, f: (f, 0)),
                  vec],
        out_specs=row,
        scratch_shapes=[pltpu.VMEM((tm, d), BF16)],
        compiler_params=_params(("arbitrary", "arbitrary")),
        name="ffn",
    )(h1, norm_w.reshape(1, d), w_gate_up, w_gate_up, wd, final_w.reshape(1, d))


def _mixer_layer(x2, batch, seq, mix_norm_w, w_in, w_gk_up, b_gk_up, gla_norm_w,
                 conv_w, w_out, ffn_norm_w, w_gate_up, w_down, final_w):
    d = x2.shape[1]
    kd = w_gk_up.shape[1]
    vd = d

    o_v = 2 * kd
    o_go = o_v + vd
    o_gk = o_go + vd
    o_b = o_gk + GK_RANK
    o_c, o_x, o_ma, o_mb = o_b + d, o_b + 2 * d, o_b + 3 * d, o_b + 4 * d
    wt = jnp.swapaxes(w_in, 0, 1)
    w_gk = jnp.pad(wt[o_gk:o_b].astype(BF16), ((0, LANES - GK_RANK), (0, 0)))
    w_up = jnp.pad(w_gk_up.astype(BF16), ((0, LANES - GK_RANK), (0, 0)))
    conv_w8 = jnp.pad(conv_w, ((0, SUBLANES - CONV_K), (0, 0)))

    hn, bcum = _norm_decay(x2, mix_norm_w, w_gk, w_up, b_gk_up.reshape(1, kd))
    qk = _in_plain(hn, wt, 0, o_v, F32)
    v = _in_plain(hn, wt, o_v, vd, BF16)
    gate_a = _in_gate_a(hn, wt, o_go, o_ma, vd)
    y_b = _in_conv(hn, wt, o_b, o_c, o_x, o_mb, d, conv_w8, seq)
    merged = _gla_merge(qk, v, bcum, gate_a, y_b, gla_norm_w.reshape(1, -1),
                        batch, seq)
    h1 = _out_proj(x2, merged, w_out.astype(BF16))
    return _ffn(h1, ffn_norm_w, w_gate_up.astype(BF16), w_down.astype(BF16),
                final_w)


def kernel(x, mix_norm_w, w_in, w_gk_up, b_gk_up, gla_norm_w, conv_w, w_out,
           ffn_norm_w, w_gate_up, w_down, final_norm_w):
    batch, seq, d = x.shape
    assert mix_norm_w.shape[0] == 1, "single-layer block"
    out = _mixer_layer(x.reshape(batch * seq, d), batch, seq, mix_norm_w[0],
                       w_in[0], w_gk_up[0], b_gk_up[0], gla_norm_w[0], conv_w[0],
                       w_out[0], ffn_norm_w[0], w_gate_up[0], w_down[0],
                       final_norm_w)
    return out.reshape(batch, seq, d)
```

```python
import functools

import jax
import jax.numpy as jnp
from jax import lax
from jax.experimental import pallas as pl
from jax.experimental.pallas import tpu as pltpu

EPS = 1e-6
GLA_HEADS = 4
GK_RANK = 16
GATE_LOGIT_NORMALIZER = 16.0
CHUNK = 64
PAIR = 2 * CHUNK
CUM_ROWS = PAIR
CONV_K = 3
ROW_SUB = 256
ROW_SUB_LAST = 128

LANES = 128
SUBLANES = 8
VMEM_LIMIT = 56 * 1024 * 1024

F32 = jnp.float32
BF16 = jnp.bfloat16


def _params(semantics):
    return pltpu.CompilerParams(dimension_semantics=semantics,
                                vmem_limit_bytes=VMEM_LIMIT)


def _sigmoid(x):
    return 1.0 / (1.0 + jnp.exp(-x))


def _dot(a, b):
    return jnp.dot(a, b, preferred_element_type=F32)


def _norm_decay_kernel(x_ref, nw_ref, wgk_ref, wup_ref, b_ref, hn_ref, o_ref, *, tm):
    row = lax.broadcasted_iota(jnp.int32, (CUM_ROWS, CUM_ROWS), 0)
    col = lax.broadcasted_iota(jnp.int32, (CUM_ROWS, CUM_ROWS), 1)
    tri = ((row >= col) & (row // CHUNK == col // CHUNK)).astype(BF16)
    lows = []
    for r in range(0, tm, CUM_ROWS):
        rows = slice(r, r + CUM_ROWS)
        x = x_ref[rows, :]
        y = x * lax.rsqrt(jnp.mean(x * x, axis=-1, keepdims=True) + EPS)
        hn = (y * nw_ref[...]).astype(BF16)
        hn_ref[rows, :] = hn
        lows.append(lax.dot_general(hn, wgk_ref[...], (((1,), (1,)), ((), ())),
                                    preferred_element_type=F32).astype(BF16))
    gks = [_dot(low, wup_ref[...]) + b_ref[...] for low in lows]
    for r, gk in zip(range(0, tm, CUM_ROWS), gks):
        rows = slice(r, r + CUM_ROWS)
        log_sig = jnp.minimum(gk, 0.0) - jnp.log(1.0 + jnp.exp(-jnp.abs(gk)))
        g = log_sig * (1.0 / GATE_LOGIT_NORMALIZER)
        g1 = g.astype(BF16)
        rem = g - g1.astype(F32)
        g2 = rem.astype(BF16)
        g3 = (rem - g2.astype(F32)).astype(BF16)
        o_ref[rows, :] = _dot(tri, g1) + _dot(tri, g2) + _dot(tri, g3)


def _norm_decay(x, norm_w, wgk, wup, b, tm=1024):
    m, d = x.shape
    r = wgk.shape[0]
    n = wup.shape[1]
    return pl.pallas_call(
        functools.partial(_norm_decay_kernel, tm=tm),
        out_shape=(jax.ShapeDtypeStruct((m, d), BF16),
                   jax.ShapeDtypeStruct((m, n), F32)),
        grid=(m // tm,),
        in_specs=[pl.BlockSpec((tm, d), lambda i: (i, 0)),
                  pl.BlockSpec((1, d), lambda i: (0, 0)),
                  pl.BlockSpec((r, d), lambda i: (0, 0)),
                  pl.BlockSpec((r, n), lambda i: (0, 0)),
                  pl.BlockSpec((1, n), lambda i: (0, 0))],
        out_specs=(pl.BlockSpec((tm, d), lambda i: (i, 0)),
                   pl.BlockSpec((tm, n), lambda i: (i, 0))),
        compiler_params=_params(("arbitrary",)),
        name="norm_decay",
    )(x, norm_w.reshape(1, d), wgk, wup, b)


def _weight_spec(d, tn, col0):
    assert col0 % SUBLANES == 0
    return pl.BlockSpec((pl.Element(tn), pl.Element(d)),
                        lambda j, i: (pl.multiple_of(col0 + j * tn, SUBLANES), 0))


def _load_weight(wt_ref):
    return wt_ref[...].T.astype(BF16)


def _row_subtiles(tm):
    r = 0
    while tm - r > ROW_SUB:
        yield r, ROW_SUB
        r += ROW_SUB
    while r < tm:
        yield r, ROW_SUB_LAST
        r += ROW_SUB_LAST


def _hn_spec(tm, d):
    return pl.BlockSpec((tm, d), lambda j, i: (i, 0))


def _out_spec(tm, tn):
    return pl.BlockSpec((tm, tn), lambda j, i: (i, j))


def _plain_kernel(h_ref, wt_ref, o_ref, wbf_ref, *, tm):
    @pl.when(pl.program_id(1) == 0)
    def _():
        wbf_ref[...] = _load_weight(wt_ref)

    for r in range(0, tm, 2 * ROW_SUB):
        rows = slice(r, r + 2 * ROW_SUB)
        o_ref[rows, :] = _dot(h_ref[rows, :], wbf_ref[...]).astype(o_ref.dtype)


def _in_plain(hn, wt, col0, n, out_dtype, tm=2048, tn=1024):
    m, d = hn.shape
    return pl.pallas_call(
        functools.partial(_plain_kernel, tm=tm),
        out_shape=jax.ShapeDtypeStruct((m, n), out_dtype),
        grid=(n // tn, m // tm),
        in_specs=[_hn_spec(tm, d), _weight_spec(d, tn, col0)],
        out_specs=_out_spec(tm, tn),
        scratch_shapes=[pltpu.VMEM((d, tn), BF16)],
        compiler_params=_params(("arbitrary", "arbitrary")),
        name="in_plain",
    )(hn, wt)


def _gate_a_kernel(h_ref, gt_ref, mt_ref, o_ref, wg_ref, wm_ref, *, tm):
    @pl.when(pl.program_id(1) == 0)
    def _():
        wg_ref[...] = _load_weight(gt_ref)
        wm_ref[...] = _load_weight(mt_ref)

    for r, n in _row_subtiles(tm):
        h = h_ref[r:r + n, :]
        g = _dot(h, wg_ref[...])
        m = _dot(h, wm_ref[...])
        o_ref[r:r + n, :] = (g * _sigmoid(g) * _sigmoid(m)).astype(o_ref.dtype)


def _in_gate_a(hn, wt, g0, m0, n, tm=2048, tn=512):
    m, d = hn.shape
    return pl.pallas_call(
        functools.partial(_gate_a_kernel, tm=tm),
        out_shape=jax.ShapeDtypeStruct((m, n), BF16),
        grid=(n // tn, m // tm),
        in_specs=[_hn_spec(tm, d), _weight_spec(d, tn, g0), _weight_spec(d, tn, m0)],
        out_specs=_out_spec(tm, tn),
        scratch_shapes=[pltpu.VMEM((d, tn), BF16)] * 2,
        compiler_params=_params(("arbitrary", "arbitrary")),
        name="in_gate_a",
    )(hn, wt, wt)


def _conv_kernel(h_ref, bt_ref, ct_ref, xt_ref, mt_ref, cw_ref, o_ref,
                 wb_ref, wc_ref, wx_ref, wm_ref, work_ref, halo_ref,
                 *, tm, tiles_per_seq):
    i = pl.program_id(1)

    @pl.when(i == 0)
    def _():
        wb_ref[...] = _load_weight(bt_ref)
        wc_ref[...] = _load_weight(ct_ref)
        wx_ref[...] = _load_weight(xt_ref)
        wm_ref[...] = _load_weight(mt_ref)

    @pl.when(i % tiles_per_seq == 0)
    def _():
        work_ref[0:SUBLANES, :] = jnp.zeros((SUBLANES, work_ref.shape[1]), F32)

    @pl.when(i % tiles_per_seq != 0)
    def _():
        work_ref[0:SUBLANES, :] = halo_ref[...]

    cw = cw_ref[...]
    for r, n in _row_subtiles(tm):
        h = h_ref[r:r + n, :]
        u = _dot(h, wc_ref[...]) * _dot(h, wx_ref[...])
        work_ref[SUBLANES + r:SUBLANES + r + n, :] = u
        u1 = work_ref[SUBLANES - 1 + r:SUBLANES - 1 + r + n, :]
        u2 = work_ref[SUBLANES - 2 + r:SUBLANES - 2 + r + n, :]
        conv = cw[0:1, :] * u2 + cw[1:2, :] * u1 + cw[2:3, :] * u
        y_b = _dot(h, wb_ref[...]) * conv
        o_ref[r:r + n, :] = (
            _sigmoid(_dot(h, wm_ref[...])) * y_b).astype(o_ref.dtype)
    halo_ref[...] = work_ref[tm:tm + SUBLANES, :]


def _in_conv(hn, wt, b0, c0, x0, m0, n, conv_w, seq, tm=2048, tn=256):
    m, d = hn.shape
    kern = functools.partial(_conv_kernel, tm=tm, tiles_per_seq=seq // tm)
    return pl.pallas_call(
        kern,
        out_shape=jax.ShapeDtypeStruct((m, n), BF16),
        grid=(n // tn, m // tm),
        in_specs=[_hn_spec(tm, d)]
                 + [_weight_spec(d, tn, c) for c in (b0, c0, x0, m0)]
                 + [pl.BlockSpec((SUBLANES, tn), lambda j, i: (0, j))],
        out_specs=_out_spec(tm, tn),
        scratch_shapes=[pltpu.VMEM((d, tn), BF16)] * 4
                       + [pltpu.VMEM((tm + SUBLANES, tn), F32),
                          pltpu.VMEM((SUBLANES, tn), F32)],
        compiler_params=_params(("arbitrary", "arbitrary")),
        name="in_conv",
    )(hn, wt, wt, wt, wt, conv_w)


def _gla_kernel(q_ref, k_ref, v_ref, b_ref, ga_ref, yb_ref, nw_ref, o_ref,
                state_ref, *, rows, dk, dv):
    @pl.when(pl.program_id(1) == 0)
    def _():
        state_ref[...] = jnp.zeros(state_ref.shape, F32)

    row = lax.broadcasted_iota(jnp.int32, (CHUNK, PAIR), 0)
    col = lax.broadcasted_iota(jnp.int32, (CHUNK, PAIR), 1)
    mask_a = (row >= col)[:, :CHUNK]
    mask_b = col <= row + CHUNK
    scale = dk ** -0.5
    nw = nw_ref[...]
    nt = (((1,), (1,)), ((), ()))
    tn = (((0,), (0,)), ((), ()))

    def pair_body(c, carry):
        r0 = pl.multiple_of(c * PAIR, PAIR)
        rs = pl.ds(r0, PAIR)
        heads = range(GLA_HEADS)
        s_as, s_bs, q_ints, k_sts, decay_cols = [], [], [], [], []
        for h in heads:
            ks = slice(h * dk, (h + 1) * dk)
            b = b_ref[rs, ks]
            b_last_a = b[CHUNK - 1:CHUNK, :]
            b_last_b = b[PAIR - 1:PAIR, :]
            b_end = jnp.concatenate([jnp.broadcast_to(b_last_a, (CHUNK, dk)),
                                     jnp.broadcast_to(b_last_b, (CHUNK, dk))], axis=0)
            q = q_ref[rs, ks].astype(F32)
            k = k_ref[rs, ks].astype(F32)
            qd = q * (scale * jnp.exp(b))
            ke = k * jnp.exp(b_end - b)
            q_dec = qd.astype(BF16)
            k_dec = (k * jnp.exp(-b)).astype(BF16)
            k_end = ke.astype(BF16)
            q_ints.append(jnp.concatenate(
                [q_dec[:CHUNK], (qd[CHUNK:] * jnp.exp(b_last_a)).astype(BF16)], axis=0))
            k_sts.append(jnp.concatenate(
                [(ke[:CHUNK] * jnp.exp(b_last_b)).astype(BF16), k_end[CHUNK:]], axis=0))
            k_x = jnp.concatenate([k_end[:CHUNK], k_dec[CHUNK:]], axis=0)
            s_as.append(lax.dot_general(q_dec[:CHUNK], k_dec[:CHUNK], nt,
                                        preferred_element_type=F32))
            s_bs.append(lax.dot_general(q_dec[CHUNK:], k_x, nt,
                                        preferred_element_type=F32))
            decay_cols.append(jnp.transpose(
                jnp.broadcast_to(jnp.exp(b_last_a + b_last_b), (LANES, dk))))
        for h in heads:
            vs = slice(h * dv, (h + 1) * dv)
            v = v_ref[rs, vs]
            o_inter = _dot(q_ints[h], state_ref[h].astype(BF16))
            kv = lax.dot_general(k_sts[h], v, tn, preferred_element_type=F32)
            p_a = jnp.where(mask_a, s_as[h], 0.0).astype(BF16)
            p_b = jnp.where(mask_b, s_bs[h], 0.0).astype(BF16)
            o = jnp.concatenate([_dot(p_a, v[:CHUNK]), _dot(p_b, v)], axis=0) + o_inter
            decay = jnp.concatenate([decay_cols[h]] * (dv // LANES), axis=1)
            state_ref[h] = state_ref[h] * decay + kv
            y = o * lax.rsqrt(jnp.mean(o * o, axis=-1, keepdims=True) + EPS) * nw
            y = y.astype(BF16)
            o_ref[rs, vs] = y * ga_ref[rs, vs] + yb_ref[rs, vs]
        return carry

    lax.fori_loop(0, rows // PAIR, pair_body, 0, unroll=2)


def _gla_merge(qkv, bcum, gate_a, y_b, norm_w, batch, seq, rows=512):
    m, vd = gate_a.shape
    kd = bcum.shape[1]
    dk = kd // GLA_HEADS
    dv = vd // GLA_HEADS
    nblk = seq // rows
    assert qkv.shape[1] == 2 * kd + vd and 2 * kd == vd
    row_map = lambda b, s: (b * nblk + s, 0)
    kern = functools.partial(_gla_kernel, rows=rows, dk=dk, dv=dv)
    return pl.pallas_call(
        kern,
        out_shape=jax.ShapeDtypeStruct((m, vd), BF16),
        grid=(batch, nblk),
        in_specs=[pl.BlockSpec((rows, kd), lambda b, s: (b * nblk + s, 0)),
                  pl.BlockSpec((rows, kd), lambda b, s: (b * nblk + s, 1)),
                  pl.BlockSpec((rows, vd), lambda b, s: (b * nblk + s, 1)),
                  pl.BlockSpec((rows, kd), row_map),
                  pl.BlockSpec((rows, vd), row_map),
                  pl.BlockSpec((rows, vd), row_map),
                  pl.BlockSpec((1, dv), lambda b, s: (0, 0))],
        out_specs=pl.BlockSpec((rows, vd), row_map),
        scratch_shapes=[pltpu.VMEM((GLA_HEADS, dk, dv), F32)],
        compiler_params=_params(("arbitrary", "arbitrary")),
        name="gla_merge",
    )(qkv, qkv, qkv, bcum, gate_a, y_b, norm_w)


def _out_proj_kernel(x_ref, m_ref, w_ref, o_ref):
    o_ref[...] = x_ref[...] + _dot(m_ref[...], w_ref[...])


def _out_proj(x, merged, w, tm=512):
    m, d = x.shape
    row = pl.BlockSpec((tm, d), lambda i: (i, 0))
    return pl.pallas_call(
        _out_proj_kernel,
        out_shape=jax.ShapeDtypeStruct((m, d), F32),
        grid=(m // tm,),
        in_specs=[row, row, pl.BlockSpec((d, d), lambda i: (0, 0))],
        out_specs=row,
        compiler_params=_params(("arbitrary",)),
        name="out_proj",
    )(x, merged, w)


def _ffn_kernel(h_ref, nw_ref, wg_ref, wu_ref, wd_ref, fw_ref, o_ref, hn_ref):
    f = pl.program_id(1)

    @pl.when(f == 0)
    def _():
        h = h_ref[...]
        y = h * lax.rsqrt(jnp.mean(h * h, axis=-1, keepdims=True) + EPS)
        hn_ref[...] = (y * nw_ref[...]).astype(BF16)
        o_ref[...] = h

    hn = hn_ref[...]
    gate = _dot(hn, wg_ref[...])
    up = _dot(hn, wu_ref[...])
    act = (gate * _sigmoid(gate) * up).astype(BF16)
    o_ref[...] += _dot(act, wd_ref[...])

    @pl.when(f == pl.num_programs(1) - 1)
    def _():
        z = o_ref[...]
        y = z * lax.rsqrt(jnp.mean(z * z, axis=-1, keepdims=True) + EPS)
        o_ref[...] = y * fw_ref[...]


def _ffn(h1, norm_w, w_gate_up, wd, final_w, tm=1024, tf=512):
    m, d = h1.shape
    hidden = wd.shape[0]
    nf = hidden // tf
    row = pl.BlockSpec((tm, d), lambda i, f: (i, 0))
    vec = pl.BlockSpec((1, d), lambda i, f: (0, 0))
    return pl.pallas_call(
        _ffn_kernel,
        out_shape=jax.ShapeDtypeStruct((m, d), F32),
        grid=(m // tm, nf),
        in_specs=[row, vec,
                  pl.BlockSpec((d, tf), lambda i, f: (0, f)),
                  pl.BlockSpec((d, tf), lambda i, f: (0, nf + f)),
                  pl.BlockSpec((tf, d), lambda i, f: (f, 0)),
                  vec],
        out_specs=row,
        scratch_shapes=[pltpu.VMEM((tm, d), BF16)],
        compiler_params=_params(("arbitrary", "arbitrary")),
        name="ffn",
    )(h1, norm_w.reshape(1, d), w_gate_up, w_gate_up, wd, final_w.reshape(1, d))


def _mixer_layer(x2, batch, seq, mix_norm_w, w_in, w_gk_up, b_gk_up, gla_norm_w,
                 conv_w, w_out, ffn_norm_w, w_gate_up, w_down, final_w):
    d = x2.shape[1]
    kd = w_gk_up.shape[1]
    vd = d

    o_go = 2 * kd + vd
    o_gk = o_go + vd
    o_b = o_gk + GK_RANK
    o_c, o_x, o_ma, o_mb = o_b + d, o_b + 2 * d, o_b + 3 * d, o_b + 4 * d
    wt = jnp.swapaxes(w_in, 0, 1)
    w_gk = jnp.pad(wt[o_gk:o_b].astype(BF16), ((0, LANES - GK_RANK), (0, 0)))
    w_up = jnp.pad(w_gk_up.astype(BF16), ((0, LANES - GK_RANK), (0, 0)))
    conv_w8 = jnp.pad(conv_w, ((0, SUBLANES - CONV_K), (0, 0)))

    hn, bcum = _norm_decay(x2, mix_norm_w, w_gk, w_up, b_gk_up.reshape(1, kd))
    qkv = _in_plain(hn, wt, 0, o_go, BF16)
    gate_a = _in_gate_a(hn, wt, o_go, o_ma, vd)
    y_b = _in_conv(hn, wt, o_b, o_c, o_x, o_mb, d, conv_w8, seq)
    merged = _gla_merge(qkv, bcum, gate_a, y_b, gla_norm_w.reshape(1, -1),
                        batch, seq)
    h1 = _out_proj(x2, merged, w_out.astype(BF16))
    return _ffn(h1, ffn_norm_w, w_gate_up.astype(BF16), w_down.astype(BF16),
                final_w)


def kernel(x, mix_norm_w, w_in, w_gk_up, b_gk_up, gla_norm_w, conv_w, w_out,
           ffn_norm_w, w_gate_up, w_down, final_norm_w):
    batch, seq, d = x.shape
    assert mix_norm_w.shape[0] == 1, "single-layer block"
    out = _mixer_layer(x.reshape(batch * seq, d), batch, seq, mix_norm_w[0],
                       w_in[0], w_gk_up[0], b_gk_up[0], gla_norm_w[0], conv_w[0],
                       w_out[0], ffn_norm_w[0], w_gate_up[0], w_down[0],
                       final_norm_w)
    return out.reshape(batch, seq, d)
```

```python
import functools

import jax
import jax.numpy as jnp
from jax import lax
from jax.experimental import pallas as pl
from jax.experimental.pallas import tpu as pltpu

EPS = 1e-6
GLA_HEADS = 4
GK_RANK = 16
GATE_LOGIT_NORMALIZER = 16.0
CHUNK = 64
PAIR = 2 * CHUNK
CUM_ROWS = PAIR
CONV_K = 3
ROW_SUB = 256
ROW_SUB_LAST = 128

LANES = 128
SUBLANES = 8
VMEM_LIMIT = 56 * 1024 * 1024

F32 = jnp.float32
BF16 = jnp.bfloat16


def _params(semantics):
    return pltpu.CompilerParams(dimension_semantics=semantics,
                                vmem_limit_bytes=VMEM_LIMIT)


def _sigmoid(x):
    return 1.0 / (1.0 + jnp.exp(-x))


def _dot(a, b):
    return jnp.dot(a, b, preferred_element_type=F32)


def _norm_decay_kernel(x_ref, nw_ref, wgk_ref, wup_ref, b_ref, hn_ref, o_ref, *, tm):
    row = lax.broadcasted_iota(jnp.int32, (CUM_ROWS, CUM_ROWS), 0)
    col = lax.broadcasted_iota(jnp.int32, (CUM_ROWS, CUM_ROWS), 1)
    tri = ((row >= col) & (row // CHUNK == col // CHUNK)).astype(BF16)
    lows = []
    for r in range(0, tm, CUM_ROWS):
        rows = slice(r, r + CUM_ROWS)
        x = x_ref[rows, :]
        y = x * lax.rsqrt(jnp.mean(x * x, axis=-1, keepdims=True) + EPS)
        hn = (y * nw_ref[...]).astype(BF16)
        hn_ref[rows, :] = hn
        lows.append(lax.dot_general(hn, wgk_ref[...], (((1,), (1,)), ((), ())),
                                    preferred_element_type=F32).astype(BF16))
    gks = [_dot(low, wup_ref[...]) + b_ref[...] for low in lows]
    for r, gk in zip(range(0, tm, CUM_ROWS), gks):
        rows = slice(r, r + CUM_ROWS)
        log_sig = jnp.minimum(gk, 0.0) - jnp.log(1.0 + jnp.exp(-jnp.abs(gk)))
        g = log_sig * (1.0 / GATE_LOGIT_NORMALIZER)
        g1 = g.astype(BF16)
        rem = g - g1.astype(F32)
        g2 = rem.astype(BF16)
        g3 = (rem - g2.astype(F32)).astype(BF16)
        o_ref[rows, :] = _dot(tri, g1) + _dot(tri, g2) + _dot(tri, g3)


def _norm_decay(x, norm_w, wgk, wup, b, tm=1024):
    m, d = x.shape
    r = wgk.shape[0]
    n = wup.shape[1]
    return pl.pallas_call(
        functools.partial(_norm_decay_kernel, tm=tm),
        out_shape=(jax.ShapeDtypeStruct((m, d), BF16),
                   jax.ShapeDtypeStruct((m, n), F32)),
        grid=(m // tm,),
        in_specs=[pl.BlockSpec((tm, d), lambda i: (i, 0)),
                  pl.BlockSpec((1, d), lambda i: (0, 0)),
                  pl.BlockSpec((r, d), lambda i: (0, 0)),
                  pl.BlockSpec((r, n), lambda i: (0, 0)),
                  pl.BlockSpec((1, n), lambda i: (0, 0))],
        out_specs=(pl.BlockSpec((tm, d), lambda i: (i, 0)),
                   pl.BlockSpec((tm, n), lambda i: (i, 0))),
        compiler_params=_params(("arbitrary",)),
        name="norm_decay",
    )(x, norm_w.reshape(1, d), wgk, wup, b)


def _weight_spec(d, tn, col0):
    assert col0 % SUBLANES == 0
    return pl.BlockSpec((pl.Element(tn), pl.Element(d)),
                        lambda j, i: (pl.multiple_of(col0 + j * tn, SUBLANES), 0))


def _load_weight(wt_ref):
    return wt_ref[...].T.astype(BF16)


def _row_subtiles(tm):
    r = 0
    while tm - r > ROW_SUB:
        yield r, ROW_SUB
        r += ROW_SUB
    while r < tm:
        yield r, ROW_SUB_LAST
        r += ROW_SUB_LAST


def _hn_spec(tm, d):
    return pl.BlockSpec((tm, d), lambda j, i: (i, 0))


def _out_spec(tm, tn):
    return pl.BlockSpec((tm, tn), lambda j, i: (i, j))


def _plain_kernel(h_ref, wt_ref, o_ref, wbf_ref, *, tm):
    @pl.when(pl.program_id(1) == 0)
    def _():
        wbf_ref[...] = _load_weight(wt_ref)

    for r in range(0, tm, 2 * ROW_SUB):
        rows = slice(r, r + 2 * ROW_SUB)
        o_ref[rows, :] = _dot(h_ref[rows, :], wbf_ref[...]).astype(o_ref.dtype)


def _in_plain(hn, wt, col0, n, out_dtype, tm=2048, tn=1024):
    m, d = hn.shape
    return pl.pallas_call(
        functools.partial(_plain_kernel, tm=tm),
        out_shape=jax.ShapeDtypeStruct((m, n), out_dtype),
        grid=(n // tn, m // tm),
        in_specs=[_hn_spec(tm, d), _weight_spec(d, tn, col0)],
        out_specs=_out_spec(tm, tn),
        scratch_shapes=[pltpu.VMEM((d, tn), BF16)],
        compiler_params=_params(("arbitrary", "arbitrary")),
        name="in_plain",
    )(hn, wt)


def _gate_a_kernel(h_ref, gt_ref, mt_ref, o_ref, wg_ref, wm_ref, *, tm):
    @pl.when(pl.program_id(1) == 0)
    def _():
        wg_ref[...] = _load_weight(gt_ref)
        wm_ref[...] = _load_weight(mt_ref)

    for r, n in _row_subtiles(tm):
        h = h_ref[r:r + n, :]
        g = _dot(h, wg_ref[...])
        m = _dot(h, wm_ref[...])
        o_ref[r:r + n, :] = (g * _sigmoid(g) * _sigmoid(m)).astype(o_ref.dtype)


def _in_gate_a(hn, wt, g0, m0, n, tm=2048, tn=512):
    m, d = hn.shape
    return pl.pallas_call(
        functools.partial(_gate_a_kernel, tm=tm),
        out_shape=jax.ShapeDtypeStruct((m, n), BF16),
        grid=(n // tn, m // tm),
        in_specs=[_hn_spec(tm, d), _weight_spec(d, tn, g0), _weight_spec(d, tn, m0)],
        out_specs=_out_spec(tm, tn),
        scratch_shapes=[pltpu.VMEM((d, tn), BF16)] * 2,
        compiler_params=_params(("arbitrary", "arbitrary")),
        name="in_gate_a",
    )(hn, wt, wt)


def _conv_kernel(*refs, tm, tiles_per_seq, n_cast):
    h_ref, bt_ref, ct_ref, xt_ref, mt_ref, cw_ref = refs[:6]
    cast_in = refs[6:6 + n_cast]
    o_ref = refs[6 + n_cast]
    cast_out = refs[7 + n_cast:7 + 2 * n_cast]
    wb_ref, wc_ref, wx_ref, wm_ref, work_ref, halo_ref = refs[7 + 2 * n_cast:]
    i = pl.program_id(1)

    for src, dst in zip(cast_in, cast_out):
        dst[...] = src[...].astype(BF16)

    @pl.when(i == 0)
    def _():
        wb_ref[...] = _load_weight(bt_ref)
        wc_ref[...] = _load_weight(ct_ref)
        wx_ref[...] = _load_weight(xt_ref)
        wm_ref[...] = _load_weight(mt_ref)

    @pl.when(i % tiles_per_seq == 0)
    def _():
        work_ref[0:SUBLANES, :] = jnp.zeros((SUBLANES, work_ref.shape[1]), F32)

    @pl.when(i % tiles_per_seq != 0)
    def _():
        work_ref[0:SUBLANES, :] = halo_ref[...]

    cw = cw_ref[...]
    for r, n in _row_subtiles(tm):
        h = h_ref[r:r + n, :]
        u = _dot(h, wc_ref[...]) * _dot(h, wx_ref[...])
        work_ref[SUBLANES + r:SUBLANES + r + n, :] = u
        u1 = work_ref[SUBLANES - 1 + r:SUBLANES - 1 + r + n, :]
        u2 = work_ref[SUBLANES - 2 + r:SUBLANES - 2 + r + n, :]
        conv = cw[0:1, :] * u2 + cw[1:2, :] * u1 + cw[2:3, :] * u
        y_b = _dot(h, wb_ref[...]) * conv
        o_ref[r:r + n, :] = (
            _sigmoid(_dot(h, wm_ref[...])) * y_b).astype(o_ref.dtype)
    halo_ref[...] = work_ref[tm:tm + SUBLANES, :]


def _cast_slab_spec(w, max_steps, n_row_tiles):
    steps = max(s for s in range(1, max_steps + 1)
                if w.shape[0] % s == 0 and (w.shape[0] // s) % (2 * SUBLANES) == 0)
    return pl.BlockSpec(
        (w.shape[0] // steps, w.shape[1]),
        lambda j, i: (jnp.minimum(j * n_row_tiles + i, steps - 1), 0))


def _in_conv(hn, wt, b0, c0, x0, m0, n, conv_w, seq, cast_weights, tm=2048, tn=256):
    m, d = hn.shape
    grid = (n // tn, m // tm)
    cast_specs = [_cast_slab_spec(w, grid[0] * grid[1], grid[1]) for w in cast_weights]
    kern = functools.partial(_conv_kernel, tm=tm, tiles_per_seq=seq // tm,
                             n_cast=len(cast_weights))
    return pl.pallas_call(
        kern,
        out_shape=[jax.ShapeDtypeStruct((m, n), BF16)]
                  + [jax.ShapeDtypeStruct(w.shape, BF16) for w in cast_weights],
        grid=grid,
        in_specs=[_hn_spec(tm, d)]
                 + [_weight_spec(d, tn, c) for c in (b0, c0, x0, m0)]
                 + [pl.BlockSpec((SUBLANES, tn), lambda j, i: (0, j))]
                 + cast_specs,
        out_specs=[_out_spec(tm, tn)] + cast_specs,
        scratch_shapes=[pltpu.VMEM((d, tn), BF16)] * 4
                       + [pltpu.VMEM((tm + SUBLANES, tn), F32),
                          pltpu.VMEM((SUBLANES, tn), F32)],
        compiler_params=_params(("arbitrary", "arbitrary")),
        name="in_conv",
    )(hn, wt, wt, wt, wt, conv_w, *cast_weights)


def _gla_kernel(q_ref, k_ref, v_ref, b_ref, ga_ref, yb_ref, nw_ref, o_ref,
                state_ref, *, rows, dk, dv):
    @pl.when(pl.program_id(1) == 0)
    def _():
        state_ref[...] = jnp.zeros(state_ref.shape, F32)

    row = lax.broadcasted_iota(jnp.int32, (CHUNK, PAIR), 0)
    col = lax.broadcasted_iota(jnp.int32, (CHUNK, PAIR), 1)
    mask_a = (row >= col)[:, :CHUNK]
    mask_b = col <= row + CHUNK
    scale = dk ** -0.5
    nw = nw_ref[...]
    nt = (((1,), (1,)), ((), ()))
    tn = (((0,), (0,)), ((), ()))

    def pair_body(c, carry):
        r0 = pl.multiple_of(c * PAIR, PAIR)
        rs = pl.ds(r0, PAIR)
        heads = range(GLA_HEADS)
        s_as, s_bs, q_ints, k_sts, decay_cols = [], [], [], [], []
        for h in heads:
            ks = slice(h * dk, (h + 1) * dk)
            b = b_ref[rs, ks]
            b_last_a = b[CHUNK - 1:CHUNK, :]
            b_last_b = b[PAIR - 1:PAIR, :]
            b_end = jnp.concatenate([jnp.broadcast_to(b_last_a, (CHUNK, dk)),
                                     jnp.broadcast_to(b_last_b, (CHUNK, dk))], axis=0)
            q = q_ref[rs, ks].astype(F32)
            k = k_ref[rs, ks].astype(F32)
            qd = q * (scale * jnp.exp(b))
            ke = k * jnp.exp(b_end - b)
            q_dec = qd.astype(BF16)
            k_dec = (k * jnp.exp(-b)).astype(BF16)
            k_end = ke.astype(BF16)
            q_ints.append(jnp.concatenate(
                [q_dec[:CHUNK], (qd[CHUNK:] * jnp.exp(b_last_a)).astype(BF16)], axis=0))
            k_sts.append(jnp.concatenate(
                [(ke[:CHUNK] * jnp.exp(b_last_b)).astype(BF16), k_end[CHUNK:]], axis=0))
            k_x = jnp.concatenate([k_end[:CHUNK], k_dec[CHUNK:]], axis=0)
            s_as.append(lax.dot_general(q_dec[:CHUNK], k_dec[:CHUNK], nt,
                                        preferred_element_type=F32))
            s_bs.append(lax.dot_general(q_dec[CHUNK:], k_x, nt,
                                        preferred_element_type=F32))
            decay_cols.append(jnp.transpose(
                jnp.broadcast_to(jnp.exp(b_last_a + b_last_b), (LANES, dk))))
        for h in heads:
            vs = slice(h * dv, (h + 1) * dv)
            v = v_ref[rs, vs]
            o_inter = _dot(q_ints[h], state_ref[h].astype(BF16))
            kv = lax.dot_general(k_sts[h], v, tn, preferred_element_type=F32)
            p_a = jnp.where(mask_a, s_as[h], 0.0).astype(BF16)
            p_b = jnp.where(mask_b, s_bs[h], 0.0).astype(BF16)
            o = jnp.concatenate([_dot(p_a, v[:CHUNK]), _dot(p_b, v)], axis=0) + o_inter
            decay = jnp.concatenate([decay_cols[h]] * (dv // LANES), axis=1)
            state_ref[h] = state_ref[h] * decay + kv
            y = o * lax.rsqrt(jnp.mean(o * o, axis=-1, keepdims=True) + EPS) * nw
            y = y.astype(BF16)
            o_ref[rs, vs] = y * ga_ref[rs, vs] + yb_ref[rs, vs]
        return carry

    lax.fori_loop(0, rows // PAIR, pair_body, 0, unroll=2)


def _gla_merge(qkv, bcum, gate_a, y_b, norm_w, batch, seq, rows=512):
    m, vd = gate_a.shape
    kd = bcum.shape[1]
    dk = kd // GLA_HEADS
    dv = vd // GLA_HEADS
    nblk = seq // rows
    assert qkv.shape[1] == 2 * kd + vd and 2 * kd == vd
    row_map = lambda b, s: (b * nblk + s, 0)
    kern = functools.partial(_gla_kernel, rows=rows, dk=dk, dv=dv)
    return pl.pallas_call(
        kern,
        out_shape=jax.ShapeDtypeStruct((m, vd), BF16),
        grid=(batch, nblk),
        in_specs=[pl.BlockSpec((rows, kd), lambda b, s: (b * nblk + s, 0)),
                  pl.BlockSpec((rows, kd), lambda b, s: (b * nblk + s, 1)),
                  pl.BlockSpec((rows, vd), lambda b, s: (b * nblk + s, 1)),
                  pl.BlockSpec((rows, kd), row_map),
                  pl.BlockSpec((rows, vd), row_map),
                  pl.BlockSpec((rows, vd), row_map),
                  pl.BlockSpec((1, dv), lambda b, s: (0, 0))],
        out_specs=pl.BlockSpec((rows, vd), row_map),
        scratch_shapes=[pltpu.VMEM((GLA_HEADS, dk, dv), F32)],
        compiler_params=_params(("arbitrary", "arbitrary")),
        name="gla_merge",
    )(qkv, qkv, qkv, bcum, gate_a, y_b, norm_w)


def _out_proj_kernel(x_ref, m_ref, w_ref, o_ref):
    o_ref[...] = x_ref[...] + _dot(m_ref[...], w_ref[...])


def _out_proj(x, merged, w, tm=512):
    m, d = x.shape
    row = pl.BlockSpec((tm, d), lambda i: (i, 0))
    return pl.pallas_call(
        _out_proj_kernel,
        out_shape=jax.ShapeDtypeStruct((m, d), F32),
        grid=(m // tm,),
        in_specs=[row, row, pl.BlockSpec((d, d), lambda i: (0, 0))],
        out_specs=row,
        compiler_params=_params(("arbitrary",)),
        name="out_proj",
    )(x, merged, w)


def _ffn_kernel(h_ref, nw_ref, wg_ref, wu_ref, wd_ref, fw_ref, o_ref, hn_ref):
    f = pl.program_id(1)

    @pl.when(f == 0)
    def _():
        h = h_ref[...]
        y = h * lax.rsqrt(jnp.mean(h * h, axis=-1, keepdims=True) + EPS)
        hn_ref[...] = (y * nw_ref[...]).astype(BF16)
        o_ref[...] = h

    hn = hn_ref[...]
    gate = _dot(hn, wg_ref[...])
    up = _dot(hn, wu_ref[...])
    act = (gate * _sigmoid(gate) * up).astype(BF16)
    o_ref[...] += _dot(act, wd_ref[...])

    @pl.when(f == pl.num_programs(1) - 1)
    def _():
        z = o_ref[...]
        y = z * lax.rsqrt(jnp.mean(z * z, axis=-1, keepdims=True) + EPS)
        o_ref[...] = y * fw_ref[...]


def _ffn(h1, norm_w, w_gate_up, wd, final_w, tm=1024, tf=512):
    m, d = h1.shape
    hidden = wd.shape[0]
    nf = hidden // tf
    row = pl.BlockSpec((tm, d), lambda i, f: (i, 0))
    vec = pl.BlockSpec((1, d), lambda i, f: (0, 0))
    return pl.pallas_call(
        _ffn_kernel,
        out_shape=jax.ShapeDtypeStruct((m, d), F32),
        grid=(m // tm, nf),
        in_specs=[row, vec,
                  pl.BlockSpec((d, tf), lambda i, f: (0, f)),
                  pl.BlockSpec((d, tf), lambda i, f: (0, nf + f)),
                  pl.BlockSpec((tf, d), lambda i, f: (f, 0)),
                  vec],
        out_specs=row,
        scratch_shapes=[pltpu.VMEM((tm, d), BF16)],
        compiler_params=_params(("arbitrary", "arbitrary")),
        name="ffn",
    )(h1, norm_w.reshape(1, d), w_gate_up, w_gate_up, wd, final_w.reshape(1, d))


def _mixer_layer(x2, batch, seq, mix_norm_w, w_in, w_gk_up, b_gk_up, gla_norm_w,
                 conv_w, w_out, ffn_norm_w, w_gate_up, w_down, final_w):
    d = x2.shape[1]
    kd = w_gk_up.shape[1]
    vd = d

    o_go = 2 * kd + vd
    o_gk = o_go + vd
    o_b = o_gk + GK_RANK
    o_c, o_x, o_ma, o_mb = o_b + d, o_b + 2 * d, o_b + 3 * d, o_b + 4 * d
    wt = jnp.swapaxes(w_in, 0, 1)
    w_gk = jnp.pad(wt[o_gk:o_b].astype(BF16), ((0, LANES - GK_RANK), (0, 0)))
    w_up = jnp.pad(w_gk_up.astype(BF16), ((0, LANES - GK_RANK), (0, 0)))
    conv_w8 = jnp.pad(conv_w, ((0, SUBLANES - CONV_K), (0, 0)))

    hn, bcum = _norm_decay(x2, mix_norm_w, w_gk, w_up, b_gk_up.reshape(1, kd))
    qkv = _in_plain(hn, wt, 0, o_go, BF16)
    gate_a = _in_gate_a(hn, wt, o_go, o_ma, vd)
    y_b, w_out16, w_gate_up16, w_down16 = _in_conv(
        hn, wt, o_b, o_c, o_x, o_mb, d, conv_w8, seq, [w_out, w_gate_up, w_down])
    merged = _gla_merge(qkv, bcum, gate_a, y_b, gla_norm_w.reshape(1, -1),
                        batch, seq)
    h1 = _out_proj(x2, merged, w_out16)
    return _ffn(h1, ffn_norm_w, w_gate_up16, w_down16, final_w)


def kernel(x, mix_norm_w, w_in, w_gk_up, b_gk_up, gla_norm_w, conv_w, w_out,
           ffn_norm_w, w_gate_up, w_down, final_norm_w):
    batch, seq, d = x.shape
    assert mix_norm_w.shape[0] == 1, "single-layer block"
    out = _mixer_layer(x.reshape(batch * seq, d), batch, seq, mix_norm_w[0],
                       w_in[0], w_gk_up[0], b_gk_up[0], gla_norm_w[0], conv_w[0],
                       w_out[0], ffn_norm_w[0], w_gate_up[0], w_down[0],
                       final_norm_w)
    return out.reshape(batch, seq, d)
```

```python
import functools

import jax
import jax.numpy as jnp
from jax import lax
from jax.experimental import pallas as pl
from jax.experimental.pallas import tpu as pltpu

EPS = 1e-6
GLA_HEADS = 4
GK_RANK = 16
GATE_LOGIT_NORMALIZER = 16.0
CHUNK = 64
PAIR = 2 * CHUNK
CUM_ROWS = PAIR
CONV_K = 3
ROW_SUB = 256
ROW_SUB_LAST = 128

LANES = 128
SUBLANES = 8
VMEM_LIMIT = 56 * 1024 * 1024

F32 = jnp.float32
BF16 = jnp.bfloat16


def _params(semantics):
    return pltpu.CompilerParams(dimension_semantics=semantics,
                                vmem_limit_bytes=VMEM_LIMIT)


def _sigmoid(x):
    return 1.0 / (1.0 + jnp.exp(-x))


def _dot(a, b):
    return jnp.dot(a, b, preferred_element_type=F32)


def _norm_qk_kernel(x_ref, nw_ref, wgk_ref, wup_ref, b_ref, wt_ref,
                    hn_ref, bcum_ref, qk_ref, wbf_ref, *, tm):
    @pl.when(pl.program_id(0) == 0)
    def _():
        wbf_ref[...] = wt_ref[...].T.astype(BF16)

    row = lax.broadcasted_iota(jnp.int32, (CUM_ROWS, CUM_ROWS), 0)
    col = lax.broadcasted_iota(jnp.int32, (CUM_ROWS, CUM_ROWS), 1)
    tri = ((row >= col) & (row // CHUNK == col // CHUNK)).astype(BF16)
    subtiles = range(0, tm, CUM_ROWS)
    lows = []
    for r in subtiles:
        rows = slice(r, r + CUM_ROWS)
        x = x_ref[rows, :]
        y = x * lax.rsqrt(jnp.mean(x * x, axis=-1, keepdims=True) + EPS)
        hn = (y * nw_ref[...]).astype(BF16)
        hn_ref[rows, :] = hn
        lows.append(lax.dot_general(hn, wgk_ref[...], (((1,), (1,)), ((), ())),
                                    preferred_element_type=F32).astype(BF16))
    gks = [_dot(low, wup_ref[...]) + b_ref[...] for low in lows]
    for r, gk in zip(subtiles, gks):
        rows = slice(r, r + CUM_ROWS)
        qk_ref[rows, :] = _dot(hn_ref[rows, :], wbf_ref[...]).astype(qk_ref.dtype)
        log_sig = jnp.minimum(gk, 0.0) - jnp.log(1.0 + jnp.exp(-jnp.abs(gk)))
        g = log_sig * (1.0 / GATE_LOGIT_NORMALIZER)
        g1 = g.astype(BF16)
        rem = g - g1.astype(F32)
        g2 = rem.astype(BF16)
        g3 = (rem - g2.astype(F32)).astype(BF16)
        bcum_ref[rows, :] = _dot(tri, g1) + _dot(tri, g2) + _dot(tri, g3)


def _norm_qk(x, norm_w, wgk, wup, b, wt, n_qk, tm=512):
    m, d = x.shape
    r = wgk.shape[0]
    n = wup.shape[1]
    row_block = lambda width: pl.BlockSpec((tm, width), lambda i: (i, 0))
    const_block = lambda shape: pl.BlockSpec(shape, lambda i: (0, 0))
    return pl.pallas_call(
        functools.partial(_norm_qk_kernel, tm=tm),
        out_shape=(jax.ShapeDtypeStruct((m, d), BF16),
                   jax.ShapeDtypeStruct((m, n), F32),
                   jax.ShapeDtypeStruct((m, n_qk), BF16)),
        grid=(m // tm,),
        in_specs=[row_block(d), const_block((1, d)), const_block((r, d)),
                  const_block((r, n)), const_block((1, n)),
                  pl.BlockSpec((pl.Element(n_qk), pl.Element(d)), lambda i: (0, 0),
                               pipeline_mode=pl.Buffered(1))],
        out_specs=(row_block(d), row_block(n), row_block(n_qk)),
        scratch_shapes=[pltpu.VMEM((d, n_qk), BF16)],
        compiler_params=_params(("arbitrary",)),
        name="norm_qk",
    )(x, norm_w.reshape(1, d), wgk, wup, b, wt)


def _weight_spec(d, tn, col0):
    assert col0 % SUBLANES == 0
    return pl.BlockSpec((pl.Element(tn), pl.Element(d)),
                        lambda j, i: (pl.multiple_of(col0 + j * tn, SUBLANES), 0))


def _load_weight(wt_ref):
    return wt_ref[...].T.astype(BF16)


def _row_subtiles(tm):
    r = 0
    while tm - r > ROW_SUB:
        yield r, ROW_SUB
        r += ROW_SUB
    while r < tm:
        yield r, ROW_SUB_LAST
        r += ROW_SUB_LAST


def _hn_spec(tm, d):
    return pl.BlockSpec((tm, d), lambda j, i: (i, 0))


def _out_spec(tm, tn):
    return pl.BlockSpec((tm, tn), lambda j, i: (i, j))


def _plain_kernel(h_ref, wt_ref, o_ref, wbf_ref, *, tm):
    @pl.when(pl.program_id(1) == 0)
    def _():
        wbf_ref[...] = _load_weight(wt_ref)

    for r in range(0, tm, 2 * ROW_SUB):
        rows = slice(r, r + 2 * ROW_SUB)
        o_ref[rows, :] = _dot(h_ref[rows, :], wbf_ref[...]).astype(o_ref.dtype)


def _in_plain(hn, wt, col0, n, out_dtype, tm=2048, tn=1024):
    m, d = hn.shape
    return pl.pallas_call(
        functools.partial(_plain_kernel, tm=tm),
        out_shape=jax.ShapeDtypeStruct((m, n), out_dtype),
        grid=(n // tn, m // tm),
        in_specs=[_hn_spec(tm, d), _weight_spec(d, tn, col0)],
        out_specs=_out_spec(tm, tn),
        scratch_shapes=[pltpu.VMEM((d, tn), BF16)],
        compiler_params=_params(("arbitrary", "arbitrary")),
        name="in_plain",
    )(hn, wt)


def _gate_a_kernel(h_ref, gt_ref, mt_ref, o_ref, wg_ref, wm_ref, *, tm):
    @pl.when(pl.program_id(1) == 0)
    def _():
        wg_ref[...] = _load_weight(gt_ref)
        wm_ref[...] = _load_weight(mt_ref)

    for r, n in _row_subtiles(tm):
        h = h_ref[r:r + n, :]
        g = _dot(h, wg_ref[...])
        m = _dot(h, wm_ref[...])
        o_ref[r:r + n, :] = (g * _sigmoid(g) * _sigmoid(m)).astype(o_ref.dtype)


def _in_gate_a(hn, wt, g0, m0, n, tm=2048, tn=512):
    m, d = hn.shape
    return pl.pallas_call(
        functools.partial(_gate_a_kernel, tm=tm),
        out_shape=jax.ShapeDtypeStruct((m, n), BF16),
        grid=(n // tn, m // tm),
        in_specs=[_hn_spec(tm, d), _weight_spec(d, tn, g0), _weight_spec(d, tn, m0)],
        out_specs=_out_spec(tm, tn),
        scratch_shapes=[pltpu.VMEM((d, tn), BF16)] * 2,
        compiler_params=_params(("arbitrary", "arbitrary")),
        name="in_gate_a",
    )(hn, wt, wt)


def _conv_kernel(*refs, tm, tiles_per_seq, n_cast):
    h_ref, bt_ref, ct_ref, xt_ref, mt_ref, cw_ref = refs[:6]
    cast_in = refs[6:6 + n_cast]
    o_ref = refs[6 + n_cast]
    cast_out = refs[7 + n_cast:7 + 2 * n_cast]
    wb_ref, wc_ref, wx_ref, wm_ref, work_ref, halo_ref = refs[7 + 2 * n_cast:]
    i = pl.program_id(1)

    for src, dst in zip(cast_in, cast_out):
        dst[...] = src[...].astype(BF16)

    @pl.when(i == 0)
    def _():
        wb_ref[...] = _load_weight(bt_ref)
        wc_ref[...] = _load_weight(ct_ref)
        wx_ref[...] = _load_weight(xt_ref)
        wm_ref[...] = _load_weight(mt_ref)

    @pl.when(i % tiles_per_seq == 0)
    def _():
        work_ref[0:SUBLANES, :] = jnp.zeros((SUBLANES, work_ref.shape[1]), F32)

    @pl.when(i % tiles_per_seq != 0)
    def _():
        work_ref[0:SUBLANES, :] = halo_ref[...]

    cw = cw_ref[...]
    for r, n in _row_subtiles(tm):
        h = h_ref[r:r + n, :]
        u = _dot(h, wc_ref[...]) * _dot(h, wx_ref[...])
        work_ref[SUBLANES + r:SUBLANES + r + n, :] = u
        u1 = work_ref[SUBLANES - 1 + r:SUBLANES - 1 + r + n, :]
        u2 = work_ref[SUBLANES - 2 + r:SUBLANES - 2 + r + n, :]
        conv = cw[0:1, :] * u2 + cw[1:2, :] * u1 + cw[2:3, :] * u
        y_b = _dot(h, wb_ref[...]) * conv
        o_ref[r:r + n, :] = (
            _sigmoid(_dot(h, wm_ref[...])) * y_b).astype(o_ref.dtype)
    halo_ref[...] = work_ref[tm:tm + SUBLANES, :]


def _cast_slab_spec(w, max_steps, n_row_tiles):
    steps = max(s for s in range(1, max_steps + 1)
                if w.shape[0] % s == 0 and (w.shape[0] // s) % (2 * SUBLANES) == 0)
    return pl.BlockSpec(
        (w.shape[0] // steps, w.shape[1]),
        lambda j, i: (jnp.minimum(j * n_row_tiles + i, steps - 1), 0))


def _in_conv(hn, wt, b0, c0, x0, m0, n, conv_w, seq, cast_weights, tm=2048, tn=256):
    m, d = hn.shape
    grid = (n // tn, m // tm)
    cast_specs = [_cast_slab_spec(w, grid[0] * grid[1], grid[1]) for w in cast_weights]
    kern = functools.partial(_conv_kernel, tm=tm, tiles_per_seq=seq // tm,
                             n_cast=len(cast_weights))
    return pl.pallas_call(
        kern,
        out_shape=[jax.ShapeDtypeStruct((m, n), BF16)]
                  + [jax.ShapeDtypeStruct(w.shape, BF16) for w in cast_weights],
        grid=grid,
        in_specs=[_hn_spec(tm, d)]
                 + [_weight_spec(d, tn, c) for c in (b0, c0, x0, m0)]
                 + [pl.BlockSpec((SUBLANES, tn), lambda j, i: (0, j))]
                 + cast_specs,
        out_specs=[_out_spec(tm, tn)] + cast_specs,
        scratch_shapes=[pltpu.VMEM((d, tn), BF16)] * 4
                       + [pltpu.VMEM((tm + SUBLANES, tn), F32),
                          pltpu.VMEM((SUBLANES, tn), F32)],
        compiler_params=_params(("arbitrary", "arbitrary")),
        name="in_conv",
    )(hn, wt, wt, wt, wt, conv_w, *cast_weights)


def _gla_kernel(q_ref, k_ref, v_ref, b_ref, ga_ref, yb_ref, nw_ref, o_ref,
                state_ref, *, rows, dk, dv):
    @pl.when(pl.program_id(1) == 0)
    def _():
        state_ref[...] = jnp.zeros(state_ref.shape, F32)

    row = lax.broadcasted_iota(jnp.int32, (CHUNK, PAIR), 0)
    col = lax.broadcasted_iota(jnp.int32, (CHUNK, PAIR), 1)
    mask_a = (row >= col)[:, :CHUNK]
    mask_b = col <= row + CHUNK
    scale = dk ** -0.5
    nw = nw_ref[...]
    nt = (((1,), (1,)), ((), ()))
    tn = (((0,), (0,)), ((), ()))

    def pair_body(c, carry):
        r0 = pl.multiple_of(c * PAIR, PAIR)
        rs = pl.ds(r0, PAIR)
        heads = range(GLA_HEADS)
        s_as, s_bs, q_ints, k_sts, decay_cols = [], [], [], [], []
        for h in heads:
            ks = slice(h * dk, (h + 1) * dk)
            b = b_ref[rs, ks]
            b_last_a = b[CHUNK - 1:CHUNK, :]
            b_last_b = b[PAIR - 1:PAIR, :]
            b_end = jnp.concatenate([jnp.broadcast_to(b_last_a, (CHUNK, dk)),
                                     jnp.broadcast_to(b_last_b, (CHUNK, dk))], axis=0)
            q = q_ref[rs, ks].astype(F32)
            k = k_ref[rs, ks].astype(F32)
            qd = q * (scale * jnp.exp(b))
            ke = k * jnp.exp(b_end - b)
            q_dec = qd.astype(BF16)
            k_dec = (k * jnp.exp(-b)).astype(BF16)
            k_end = ke.astype(BF16)
            q_ints.append(jnp.concatenate(
                [q_dec[:CHUNK], (qd[CHUNK:] * jnp.exp(b_last_a)).astype(BF16)], axis=0))
            k_sts.append(jnp.concatenate(
                [(ke[:CHUNK] * jnp.exp(b_last_b)).astype(BF16), k_end[CHUNK:]], axis=0))
            k_x = jnp.concatenate([k_end[:CHUNK], k_dec[CHUNK:]], axis=0)
            s_as.append(lax.dot_general(q_dec[:CHUNK], k_dec[:CHUNK], nt,
                                        preferred_element_type=F32))
            s_bs.append(lax.dot_general(q_dec[CHUNK:], k_x, nt,
                                        preferred_element_type=F32))
            decay_cols.append(jnp.transpose(
                jnp.broadcast_to(jnp.exp(b_last_a + b_last_b), (LANES, dk))))
        for h in heads:
            vs = slice(h * dv, (h + 1) * dv)
            v = v_ref[rs, vs]
            o_inter = _dot(q_ints[h], state_ref[h].astype(BF16))
            kv = lax.dot_general(k_sts[h], v, tn, preferred_element_type=F32)
            p_a = jnp.where(mask_a, s_as[h], 0.0).astype(BF16)
            p_b = jnp.where(mask_b, s_bs[h], 0.0).astype(BF16)
            o = jnp.concatenate([_dot(p_a, v[:CHUNK]), _dot(p_b, v)], axis=0) + o_inter
            decay = jnp.concatenate([decay_cols[h]] * (dv // LANES), axis=1)
            state_ref[h] = state_ref[h] * decay + kv
            y = o * lax.rsqrt(jnp.mean(o * o, axis=-1, keepdims=True) + EPS) * nw
            y = y.astype(BF16)
            o_ref[rs, vs] = y * ga_ref[rs, vs] + yb_ref[rs, vs]
        return carry

    lax.fori_loop(0, rows // PAIR, pair_body, 0, unroll=2)


def _gla_merge(qk, v, bcum, gate_a, y_b, norm_w, batch, seq, rows=512):
    m, vd = v.shape
    kd = bcum.shape[1]
    dk = kd // GLA_HEADS
    dv = vd // GLA_HEADS
    nblk = seq // rows
    assert qk.shape[1] == 2 * kd
    row_map = lambda b, s: (b * nblk + s, 0)
    kern = functools.partial(_gla_kernel, rows=rows, dk=dk, dv=dv)
    return pl.pallas_call(
        kern,
        out_shape=jax.ShapeDtypeStruct((m, vd), BF16),
        grid=(batch, nblk),
        in_specs=[pl.BlockSpec((rows, kd), lambda b, s: (b * nblk + s, 0)),
                  pl.BlockSpec((rows, kd), lambda b, s: (b * nblk + s, 1)),
                  pl.BlockSpec((rows, vd), row_map),
                  pl.BlockSpec((rows, kd), row_map),
                  pl.BlockSpec((rows, vd), row_map),
                  pl.BlockSpec((rows, vd), row_map),
                  pl.BlockSpec((1, dv), lambda b, s: (0, 0))],
        out_specs=pl.BlockSpec((rows, vd), row_map),
        scratch_shapes=[pltpu.VMEM((GLA_HEADS, dk, dv), F32)],
        compiler_params=_params(("arbitrary", "arbitrary")),
        name="gla_merge",
    )(qk, qk, v, bcum, gate_a, y_b, norm_w)


def _out_proj_kernel(x_ref, m_ref, w_ref, o_ref):
    o_ref[...] = x_ref[...] + _dot(m_ref[...], w_ref[...])


def _out_proj(x, merged, w, tm=512):
    m, d = x.shape
    row = pl.BlockSpec((tm, d), lambda i: (i, 0))
    return pl.pallas_call(
        _out_proj_kernel,
        out_shape=jax.ShapeDtypeStruct((m, d), F32),
        grid=(m // tm,),
        in_specs=[row, row, pl.BlockSpec((d, d), lambda i: (0, 0))],
        out_specs=row,
        compiler_params=_params(("arbitrary",)),
        name="out_proj",
    )(x, merged, w)


def _ffn_kernel(h_ref, nw_ref, wg_ref, wu_ref, wd_ref, fw_ref, o_ref, hn_ref):
    f = pl.program_id(1)

    @pl.when(f == 0)
    def _():
        h = h_ref[...]
        y = h * lax.rsqrt(jnp.mean(h * h, axis=-1, keepdims=True) + EPS)
        hn_ref[...] = (y * nw_ref[...]).astype(BF16)
        o_ref[...] = h

    hn = hn_ref[...]
    gate = _dot(hn, wg_ref[...])
    up = _dot(hn, wu_ref[...])
    act = (gate * _sigmoid(gate) * up).astype(BF16)
    o_ref[...] += _dot(act, wd_ref[...])

    @pl.when(f == pl.num_programs(1) - 1)
    def _():
        z = o_ref[...]
        y = z * lax.rsqrt(jnp.mean(z * z, axis=-1, keepdims=True) + EPS)
        o_ref[...] = y * fw_ref[...]


def _ffn(h1, norm_w, w_gate_up, wd, final_w, tm=1024, tf=512):
    m, d = h1.shape
    hidden = wd.shape[0]
    nf = hidden // tf
    row = pl.BlockSpec((tm, d), lambda i, f: (i, 0))
    vec = pl.BlockSpec((1, d), lambda i, f: (0, 0))
    return pl.pallas_call(
        _ffn_kernel,
        out_shape=jax.ShapeDtypeStruct((m, d), F32),
        grid=(m // tm, nf),
        in_specs=[row, vec,
                  pl.BlockSpec((d, tf), lambda i, f: (0, f)),
                  pl.BlockSpec((d, tf), lambda i, f: (0, nf + f)),
                  pl.BlockSpec((tf, d), lambda i, f: (f, 0)),
                  vec],
        out_specs=row,
        scratch_shapes=[pltpu.VMEM((tm, d), BF16)],
        compiler_params=_params(("arbitrary", "arbitrary")),
        name="ffn",
    )(h1, norm_w.reshape(1, d), w_gate_up, w_gate_up, wd, final_w.reshape(1, d))


def _mixer_layer(x2, batch, seq, mix_norm_w, w_in, w_gk_up, b_gk_up, gla_norm_w,
                 conv_w, w_out, ffn_norm_w, w_gate_up, w_down, final_w):
    d = x2.shape[1]
    kd = w_gk_up.shape[1]
    vd = d

    o_go = 2 * kd + vd
    o_gk = o_go + vd
    o_b = o_gk + GK_RANK
    o_c, o_x, o_ma, o_mb = o_b + d, o_b + 2 * d, o_b + 3 * d, o_b + 4 * d
    wt = jnp.swapaxes(w_in, 0, 1)
    w_gk = jnp.pad(wt[o_gk:o_b].astype(BF16), ((0, LANES - GK_RANK), (0, 0)))
    w_up = jnp.pad(w_gk_up.astype(BF16), ((0, LANES - GK_RANK), (0, 0)))
    conv_w8 = jnp.pad(conv_w, ((0, SUBLANES - CONV_K), (0, 0)))

    hn, bcum, qk = _norm_qk(x2, mix_norm_w, w_gk, w_up, b_gk_up.reshape(1, kd),
                            wt, 2 * kd)
    v = _in_plain(hn, wt, 2 * kd, vd, BF16)
    gate_a = _in_gate_a(hn, wt, o_go, o_ma, vd)
    y_b, w_out16, w_gate_up16, w_down16 = _in_conv(
        hn, wt, o_b, o_c, o_x, o_mb, d, conv_w8, seq, [w_out, w_gate_up, w_down])
    merged = _gla_merge(qk, v, bcum, gate_a, y_b, gla_norm_w.reshape(1, -1),
                        batch, seq)
    h1 = _out_proj(x2, merged, w_out16)
    return _ffn(h1, ffn_norm_w, w_gate_up16, w_down16, final_w)


def kernel(x, mix_norm_w, w_in, w_gk_up, b_gk_up, gla_norm_w, conv_w, w_out,
           ffn_norm_w, w_gate_up, w_down, final_norm_w):
    batch, seq, d = x.shape
    assert mix_norm_w.shape[0] == 1, "single-layer block"
    out = _mixer_layer(x.reshape(batch * seq, d), batch, seq, mix_norm_w[0],
                       w_in[0], w_gk_up[0], b_gk_up[0], gla_norm_w[0], conv_w[0],
                       w_out[0], ffn_norm_w[0], w_gate_up[0], w_down[0],
                       final_norm_w)
    return out.reshape(batch, seq, d)
```

```python
import functools

import jax
import jax.numpy as jnp
from jax import lax
from jax.experimental import pallas as pl
from jax.experimental.pallas import tpu as pltpu

EPS = 1e-6
GLA_HEADS = 4
GK_RANK = 16
GATE_LOGIT_NORMALIZER = 16.0
CHUNK = 64
PAIR = 2 * CHUNK
CUM_ROWS = PAIR
OUT_ROWS = 2 * PAIR
CONV_K = 3
ROW_SUB = 256
ROW_SUB_LAST = 128

LANES = 128
SUBLANES = 8
VMEM_LIMIT = 56 * 1024 * 1024

F32 = jnp.float32
BF16 = jnp.bfloat16


def _params(semantics):
    return pltpu.CompilerParams(dimension_semantics=semantics,
                                vmem_limit_bytes=VMEM_LIMIT)


def _sigmoid(x):
    return 1.0 / (1.0 + jnp.exp(-x))


def _dot(a, b):
    return jnp.dot(a, b, preferred_element_type=F32)


def _norm_qk_kernel(x_ref, nw_ref, wgk_ref, wup_ref, b_ref, wt_ref,
                    hn_ref, bcum_ref, qk_ref, wbf_ref, *, tm):
    @pl.when(pl.program_id(0) == 0)
    def _():
        wbf_ref[...] = wt_ref[...].T.astype(BF16)

    row = lax.broadcasted_iota(jnp.int32, (CUM_ROWS, CUM_ROWS), 0)
    col = lax.broadcasted_iota(jnp.int32, (CUM_ROWS, CUM_ROWS), 1)
    tri = ((row >= col) & (row // CHUNK == col // CHUNK)).astype(BF16)
    subtiles = range(0, tm, CUM_ROWS)
    lows = []
    for r in subtiles:
        rows = slice(r, r + CUM_ROWS)
        x = x_ref[rows, :]
        y = x * lax.rsqrt(jnp.mean(x * x, axis=-1, keepdims=True) + EPS)
        hn = (y * nw_ref[...]).astype(BF16)
        hn_ref[rows, :] = hn
        lows.append(lax.dot_general(hn, wgk_ref[...], (((1,), (1,)), ((), ())),
                                    preferred_element_type=F32).astype(BF16))
    gks = [_dot(low, wup_ref[...]) + b_ref[...] for low in lows]
    for r, gk in zip(subtiles, gks):
        rows = slice(r, r + CUM_ROWS)
        qk_ref[rows, :] = _dot(hn_ref[rows, :], wbf_ref[...]).astype(qk_ref.dtype)
        log_sig = jnp.minimum(gk, 0.0) - jnp.log(1.0 + jnp.exp(-jnp.abs(gk)))
        g = log_sig * (1.0 / GATE_LOGIT_NORMALIZER)
        g1 = g.astype(BF16)
        rem = g - g1.astype(F32)
        g2 = rem.astype(BF16)
        g3 = (rem - g2.astype(F32)).astype(BF16)
        bcum_ref[rows, :] = _dot(tri, g1) + _dot(tri, g2) + _dot(tri, g3)


def _norm_qk(x, norm_w, wgk, wup, b, wt, n_qk, tm=512):
    m, d = x.shape
    r = wgk.shape[0]
    n = wup.shape[1]
    row_block = lambda width: pl.BlockSpec((tm, width), lambda i: (i, 0))
    const_block = lambda shape: pl.BlockSpec(shape, lambda i: (0, 0))
    return pl.pallas_call(
        functools.partial(_norm_qk_kernel, tm=tm),
        out_shape=(jax.ShapeDtypeStruct((m, d), BF16),
                   jax.ShapeDtypeStruct((m, n), F32),
                   jax.ShapeDtypeStruct((m, n_qk), BF16)),
        grid=(m // tm,),
        in_specs=[row_block(d), const_block((1, d)), const_block((r, d)),
                  const_block((r, n)), const_block((1, n)),
                  pl.BlockSpec((pl.Element(n_qk), pl.Element(d)), lambda i: (0, 0),
                               pipeline_mode=pl.Buffered(1))],
        out_specs=(row_block(d), row_block(n), row_block(n_qk)),
        scratch_shapes=[pltpu.VMEM((d, n_qk), BF16)],
        compiler_params=_params(("arbitrary",)),
        name="norm_qk",
    )(x, norm_w.reshape(1, d), wgk, wup, b, wt)


def _weight_spec(d, tn, col0):
    assert col0 % SUBLANES == 0
    return pl.BlockSpec((pl.Element(tn), pl.Element(d)),
                        lambda j, i: (pl.multiple_of(col0 + j * tn, SUBLANES), 0))


def _load_weight(wt_ref):
    return wt_ref[...].T.astype(BF16)


def _row_subtiles(tm):
    r = 0
    while tm - r > ROW_SUB:
        yield r, ROW_SUB
        r += ROW_SUB
    while r < tm:
        yield r, ROW_SUB_LAST
        r += ROW_SUB_LAST


def _hn_spec(tm, d):
    return pl.BlockSpec((tm, d), lambda j, i: (i, 0))


def _out_spec(tm, tn):
    return pl.BlockSpec((tm, tn), lambda j, i: (i, j))


def _plain_kernel(h_ref, wt_ref, o_ref, wbf_ref, *, tm):
    @pl.when(pl.program_id(1) == 0)
    def _():
        wbf_ref[...] = _load_weight(wt_ref)

    for r in range(0, tm, 2 * ROW_SUB):
        rows = slice(r, r + 2 * ROW_SUB)
        o_ref[rows, :] = _dot(h_ref[rows, :], wbf_ref[...]).astype(o_ref.dtype)


def _in_plain(hn, wt, col0, n, out_dtype, tm=2048, tn=1024):
    m, d = hn.shape
    return pl.pallas_call(
        functools.partial(_plain_kernel, tm=tm),
        out_shape=jax.ShapeDtypeStruct((m, n), out_dtype),
        grid=(n // tn, m // tm),
        in_specs=[_hn_spec(tm, d), _weight_spec(d, tn, col0)],
        out_specs=_out_spec(tm, tn),
        scratch_shapes=[pltpu.VMEM((d, tn), BF16)],
        compiler_params=_params(("arbitrary", "arbitrary")),
        name="in_plain",
    )(hn, wt)


def _gate_a_kernel(h_ref, gt_ref, mt_ref, o_ref, wg_ref, wm_ref, *, tm):
    @pl.when(pl.program_id(1) == 0)
    def _():
        wg_ref[...] = _load_weight(gt_ref)
        wm_ref[...] = _load_weight(mt_ref)

    for r, n in _row_subtiles(tm):
        h = h_ref[r:r + n, :]
        g = _dot(h, wg_ref[...])
        m = _dot(h, wm_ref[...])
        o_ref[r:r + n, :] = (g * _sigmoid(g) * _sigmoid(m)).astype(o_ref.dtype)


def _in_gate_a(hn, wt, g0, m0, n, tm=2048, tn=512):
    m, d = hn.shape
    return pl.pallas_call(
        functools.partial(_gate_a_kernel, tm=tm),
        out_shape=jax.ShapeDtypeStruct((m, n), BF16),
        grid=(n // tn, m // tm),
        in_specs=[_hn_spec(tm, d), _weight_spec(d, tn, g0), _weight_spec(d, tn, m0)],
        out_specs=_out_spec(tm, tn),
        scratch_shapes=[pltpu.VMEM((d, tn), BF16)] * 2,
        compiler_params=_params(("arbitrary", "arbitrary")),
        name="in_gate_a",
    )(hn, wt, wt)


def _conv_kernel(*refs, tm, tiles_per_seq, n_cast):
    h_ref, bt_ref, ct_ref, xt_ref, mt_ref, cw_ref = refs[:6]
    cast_in = refs[6:6 + n_cast]
    o_ref = refs[6 + n_cast]
    cast_out = refs[7 + n_cast:7 + 2 * n_cast]
    wb_ref, wc_ref, wx_ref, wm_ref, work_ref, halo_ref = refs[7 + 2 * n_cast:]
    i = pl.program_id(1)

    for src, dst in zip(cast_in, cast_out):
        dst[...] = src[...].astype(BF16)

    @pl.when(i == 0)
    def _():
        wb_ref[...] = _load_weight(bt_ref)
        wc_ref[...] = _load_weight(ct_ref)
        wx_ref[...] = _load_weight(xt_ref)
        wm_ref[...] = _load_weight(mt_ref)

    @pl.when(i % tiles_per_seq == 0)
    def _():
        work_ref[0:SUBLANES, :] = jnp.zeros((SUBLANES, work_ref.shape[1]), F32)

    @pl.when(i % tiles_per_seq != 0)
    def _():
        work_ref[0:SUBLANES, :] = halo_ref[...]

    cw = cw_ref[...]
    for r, n in _row_subtiles(tm):
        h = h_ref[r:r + n, :]
        u = _dot(h, wc_ref[...]) * _dot(h, wx_ref[...])
        work_ref[SUBLANES + r:SUBLANES + r + n, :] = u
        u1 = work_ref[SUBLANES - 1 + r:SUBLANES - 1 + r + n, :]
        u2 = work_ref[SUBLANES - 2 + r:SUBLANES - 2 + r + n, :]
        conv = cw[0:1, :] * u2 + cw[1:2, :] * u1 + cw[2:3, :] * u
        y_b = _dot(h, wb_ref[...]) * conv
        o_ref[r:r + n, :] = (
            _sigmoid(_dot(h, wm_ref[...])) * y_b).astype(o_ref.dtype)
    halo_ref[...] = work_ref[tm:tm + SUBLANES, :]


def _cast_slab_spec(w, max_steps, n_row_tiles):
    steps = max(s for s in range(1, max_steps + 1)
                if w.shape[0] % s == 0 and (w.shape[0] // s) % (2 * SUBLANES) == 0)
    return pl.BlockSpec(
        (w.shape[0] // steps, w.shape[1]),
        lambda j, i: (jnp.minimum(j * n_row_tiles + i, steps - 1), 0))


def _in_conv(hn, wt, b0, c0, x0, m0, n, conv_w, seq, cast_weights, tm=2048, tn=256):
    m, d = hn.shape
    grid = (n // tn, m // tm)
    cast_specs = [_cast_slab_spec(w, grid[0] * grid[1], grid[1]) for w in cast_weights]
    kern = functools.partial(_conv_kernel, tm=tm, tiles_per_seq=seq // tm,
                             n_cast=len(cast_weights))
    return pl.pallas_call(
        kern,
        out_shape=[jax.ShapeDtypeStruct((m, n), BF16)]
                  + [jax.ShapeDtypeStruct(w.shape, BF16) for w in cast_weights],
        grid=grid,
        in_specs=[_hn_spec(tm, d)]
                 + [_weight_spec(d, tn, c) for c in (b0, c0, x0, m0)]
                 + [pl.BlockSpec((SUBLANES, tn), lambda j, i: (0, j))]
                 + cast_specs,
        out_specs=[_out_spec(tm, tn)] + cast_specs,
        scratch_shapes=[pltpu.VMEM((d, tn), BF16)] * 4
                       + [pltpu.VMEM((tm + SUBLANES, tn), F32),
                          pltpu.VMEM((SUBLANES, tn), F32)],
        compiler_params=_params(("arbitrary", "arbitrary")),
        name="in_conv",
    )(hn, wt, wt, wt, wt, conv_w, *cast_weights)


def _gla_kernel(q_ref, k_ref, v_ref, b_ref, ga_ref, yb_ref, nw_ref, x_ref, wo_ref,
                o_ref, state_ref, merged_ref, *, rows, dk, dv):
    @pl.when(pl.program_id(1) == 0)
    def _():
        state_ref[...] = jnp.zeros(state_ref.shape, F32)

    row = lax.broadcasted_iota(jnp.int32, (CHUNK, PAIR), 0)
    col = lax.broadcasted_iota(jnp.int32, (CHUNK, PAIR), 1)
    mask_a = (row >= col)[:, :CHUNK]
    mask_b = col <= row + CHUNK
    scale = dk ** -0.5
    nw = nw_ref[...]
    nt = (((1,), (1,)), ((), ()))
    tn = (((0,), (0,)), ((), ()))

    def pair_body(c):
        rs = slice(c * PAIR, (c + 1) * PAIR)
        heads = range(GLA_HEADS)
        s_as, s_bs, q_ints, k_sts, decay_cols = [], [], [], [], []
        for h in heads:
            ks = slice(h * dk, (h + 1) * dk)
            b = b_ref[rs, ks]
            b_last_a = b[CHUNK - 1:CHUNK, :]
            b_last_b = b[PAIR - 1:PAIR, :]
            b_end = jnp.concatenate([jnp.broadcast_to(b_last_a, (CHUNK, dk)),
                                     jnp.broadcast_to(b_last_b, (CHUNK, dk))], axis=0)
            q = q_ref[rs, ks].astype(F32)
            k = k_ref[rs, ks].astype(F32)
            qd = q * (scale * jnp.exp(b))
            ke = k * jnp.exp(b_end - b)
            q_dec = qd.astype(BF16)
            k_dec = (k * jnp.exp(-b)).astype(BF16)
            k_end = ke.astype(BF16)
            q_ints.append(jnp.concatenate(
                [q_dec[:CHUNK], (qd[CHUNK:] * jnp.exp(b_last_a)).astype(BF16)], axis=0))
            k_sts.append(jnp.concatenate(
                [(ke[:CHUNK] * jnp.exp(b_last_b)).astype(BF16), k_end[CHUNK:]], axis=0))
            k_x = jnp.concatenate([k_end[:CHUNK], k_dec[CHUNK:]], axis=0)
            s_as.append(lax.dot_general(q_dec[:CHUNK], k_dec[:CHUNK], nt,
                                        preferred_element_type=F32))
            s_bs.append(lax.dot_general(q_dec[CHUNK:], k_x, nt,
                                        preferred_element_type=F32))
            decay_cols.append(jnp.transpose(
                jnp.broadcast_to(jnp.exp(b_last_a + b_last_b), (LANES, dk))))
        for h in heads:
            vs = slice(h * dv, (h + 1) * dv)
            v = v_ref[rs, vs]
            o_inter = _dot(q_ints[h], state_ref[h].astype(BF16))
            kv = lax.dot_general(k_sts[h], v, tn, preferred_element_type=F32)
            p_a = jnp.where(mask_a, s_as[h], 0.0).astype(BF16)
            p_b = jnp.where(mask_b, s_bs[h], 0.0).astype(BF16)
            o = jnp.concatenate([_dot(p_a, v[:CHUNK]), _dot(p_b, v)], axis=0) + o_inter
            decay = jnp.concatenate([decay_cols[h]] * (dv // LANES), axis=1)
            state_ref[h] = state_ref[h] * decay + kv
            y = o * lax.rsqrt(jnp.mean(o * o, axis=-1, keepdims=True) + EPS) * nw
            merged_ref[rs, vs] = y.astype(BF16) * ga_ref[rs, vs] + yb_ref[rs, vs]

    for g in range(rows // OUT_ROWS):
        for c in range(g * OUT_ROWS // PAIR, (g + 1) * OUT_ROWS // PAIR):
            pair_body(c)
        r = slice(g * OUT_ROWS, (g + 1) * OUT_ROWS)
        o_ref[r, :] = x_ref[r, :] + _dot(merged_ref[r, :], wo_ref[...])


def _gla_out(qk, v, bcum, gate_a, y_b, norm_w, x, w_out, batch, seq, rows=512):
    m, vd = v.shape
    d = x.shape[1]
    assert w_out.shape == (vd, d)
    kd = bcum.shape[1]
    dk = kd // GLA_HEADS
    dv = vd // GLA_HEADS
    nblk = seq // rows
    assert qk.shape[1] == 2 * kd
    row_map = lambda b, s: (b * nblk + s, 0)
    kern = functools.partial(_gla_kernel, rows=rows, dk=dk, dv=dv)
    return pl.pallas_call(
        kern,
        out_shape=jax.ShapeDtypeStruct((m, d), F32),
        grid=(batch, nblk),
        in_specs=[pl.BlockSpec((rows, kd), lambda b, s: (b * nblk + s, 0)),
                  pl.BlockSpec((rows, kd), lambda b, s: (b * nblk + s, 1)),
                  pl.BlockSpec((rows, vd), row_map),
                  pl.BlockSpec((rows, kd), row_map),
                  pl.BlockSpec((rows, vd), row_map),
                  pl.BlockSpec((rows, vd), row_map),
                  pl.BlockSpec((1, dv), lambda b, s: (0, 0)),
                  pl.BlockSpec((rows, d), row_map),
                  pl.BlockSpec((vd, d), lambda b, s: (0, 0),
                               pipeline_mode=pl.Buffered(1))],
        out_specs=pl.BlockSpec((rows, d), row_map),
        scratch_shapes=[pltpu.VMEM((GLA_HEADS, dk, dv), F32),
                        pltpu.VMEM((rows, vd), BF16)],
        compiler_params=_params(("arbitrary", "arbitrary")),
        name="gla_out",
    )(qk, qk, v, bcum, gate_a, y_b, norm_w, x, w_out)


def _ffn_kernel(h_ref, nw_ref, wg_ref, wu_ref, wd_ref, fw_ref, o_ref, hn_ref):
    f = pl.program_id(1)

    @pl.when(f == 0)
    def _():
        h = h_ref[...]
        y = h * lax.rsqrt(jnp.mean(h * h, axis=-1, keepdims=True) + EPS)
        hn_ref[...] = (y * nw_ref[...]).astype(BF16)
        o_ref[...] = h

    hn = hn_ref[...]
    gate = _dot(hn, wg_ref[...])
    up = _dot(hn, wu_ref[...])
    act = (gate * _sigmoid(gate) * up).astype(BF16)
    o_ref[...] += _dot(act, wd_ref[...])

    @pl.when(f == pl.num_programs(1) - 1)
    def _():
        z = o_ref[...]
        y = z * lax.rsqrt(jnp.mean(z * z, axis=-1, keepdims=True) + EPS)
        o_ref[...] = y * fw_ref[...]


def _ffn(h1, norm_w, w_gate_up, wd, final_w, tm=1024, tf=512):
    m, d = h1.shape
    hidden = wd.shape[0]
    nf = hidden // tf
    row = pl.BlockSpec((tm, d), lambda i, f: (i, 0))
    vec = pl.BlockSpec((1, d), lambda i, f: (0, 0))
    return pl.pallas_call(
        _ffn_kernel,
        out_shape=jax.ShapeDtypeStruct((m, d), F32),
        grid=(m // tm, nf),
        in_specs=[row, vec,
                  pl.BlockSpec((d, tf), lambda i, f: (0, f)),
                  pl.BlockSpec((d, tf), lambda i, f: (0, nf + f)),
                  pl.BlockSpec((tf, d), lambda i, f: (f, 0)),
                  vec],
        out_specs=row,
        scratch_shapes=[pltpu.VMEM((tm, d), BF16)],
        compiler_params=_params(("arbitrary", "arbitrary")),
        name="ffn",
    )(h1, norm_w.reshape(1, d), w_gate_up, w_gate_up, wd, final_w.reshape(1, d))


def _mixer_layer(x2, batch, seq, mix_norm_w, w_in, w_gk_up, b_gk_up, gla_norm_w,
                 conv_w, w_out, ffn_norm_w, w_gate_up, w_down, final_w):
    d = x2.shape[1]
    kd = w_gk_up.shape[1]
    vd = d

    o_go = 2 * kd + vd
    o_gk = o_go + vd
    o_b = o_gk + GK_RANK
    o_c, o_x, o_ma, o_mb = o_b + d, o_b + 2 * d, o_b + 3 * d, o_b + 4 * d
    wt = jnp.swapaxes(w_in, 0, 1)
    w_gk = jnp.pad(wt[o_gk:o_b].astype(BF16), ((0, LANES - GK_RANK), (0, 0)))
    w_up = jnp.pad(w_gk_up.astype(BF16), ((0, LANES - GK_RANK), (0, 0)))
    conv_w8 = jnp.pad(conv_w, ((0, SUBLANES - CONV_K), (0, 0)))

    hn, bcum, qk = _norm_qk(x2, mix_norm_w, w_gk, w_up, b_gk_up.reshape(1, kd),
                            wt, 2 * kd)
    v = _in_plain(hn, wt, 2 * kd, vd, BF16)
    gate_a = _in_gate_a(hn, wt, o_go, o_ma, vd)
    y_b, w_out16, w_gate_up16, w_down16 = _in_conv(
        hn, wt, o_b, o_c, o_x, o_mb, d, conv_w8, seq, [w_out, w_gate_up, w_down])
    h1 = _gla_out(qk, v, bcum, gate_a, y_b, gla_norm_w.reshape(1, -1), x2, w_out16,
                  batch, seq)
    return _ffn(h1, ffn_norm_w, w_gate_up16, w_down16, final_w)


def kernel(x, mix_norm_w, w_in, w_gk_up, b_gk_up, gla_norm_w, conv_w, w_out,
           ffn_norm_w, w_gate_up, w_down, final_norm_w):
    batch, seq, d = x.shape
    assert mix_norm_w.shape[0] == 1, "single-layer block"
    out = _mixer_layer(x.reshape(batch * seq, d), batch, seq, mix_norm_w[0],
                       w_in[0], w_gk_up[0], b_gk_up[0], gla_norm_w[0], conv_w[0],
                       w_out[0], ffn_norm_w[0], w_gate_up[0], w_down[0],
                       final_norm_w)
    return out.reshape(batch, seq, d)
```

```python
import functools

import jax
import jax.numpy as jnp
from jax import lax
from jax.experimental import pallas as pl
from jax.experimental.pallas import tpu as pltpu

EPS = 1e-6
GLA_HEADS = 4
GK_RANK = 16
GATE_LOGIT_NORMALIZER = 16.0
CHUNK = 64
PAIR = 2 * CHUNK
CUM_ROWS = PAIR
OUT_ROWS = 2 * PAIR
CONV_K = 3
ROW_SUB = 256
ROW_SUB_LAST = 128

LANES = 128
SUBLANES = 8
VMEM_LIMIT = 56 * 1024 * 1024

F32 = jnp.float32
BF16 = jnp.bfloat16


def _params(semantics):
    return pltpu.CompilerParams(dimension_semantics=semantics,
                                vmem_limit_bytes=VMEM_LIMIT)


def _sigmoid(x):
    return 1.0 / (1.0 + jnp.exp(-x))


def _dot(a, b):
    return jnp.dot(a, b, preferred_element_type=F32)


def _norm_qk_kernel(x_ref, nw_ref, wgk_ref, wup_ref, b_ref, wt_ref,
                    hn_ref, bcum_ref, qk_ref, wbf_ref, *, tm):
    @pl.when(pl.program_id(0) == 0)
    def _():
        wbf_ref[...] = wt_ref[...].T.astype(BF16)

    row = lax.broadcasted_iota(jnp.int32, (CUM_ROWS, CUM_ROWS), 0)
    col = lax.broadcasted_iota(jnp.int32, (CUM_ROWS, CUM_ROWS), 1)
    tri = ((row >= col) & (row // CHUNK == col // CHUNK)).astype(BF16)
    subtiles = range(0, tm, CUM_ROWS)
    lows = []
    for r in subtiles:
        rows = slice(r, r + CUM_ROWS)
        x = x_ref[rows, :]
        y = x * lax.rsqrt(jnp.mean(x * x, axis=-1, keepdims=True) + EPS)
        hn = (y * nw_ref[...]).astype(BF16)
        hn_ref[rows, :] = hn
        lows.append(lax.dot_general(hn, wgk_ref[...], (((1,), (1,)), ((), ())),
                                    preferred_element_type=F32).astype(BF16))
    gks = [_dot(low, wup_ref[...]) + b_ref[...] for low in lows]
    for r, gk in zip(subtiles, gks):
        rows = slice(r, r + CUM_ROWS)
        qk_ref[rows, :] = _dot(hn_ref[rows, :], wbf_ref[...]).astype(qk_ref.dtype)
        log_sig = jnp.minimum(gk, 0.0) - jnp.log(1.0 + jnp.exp(-jnp.abs(gk)))
        g = log_sig * (1.0 / GATE_LOGIT_NORMALIZER)
        g1 = g.astype(BF16)
        rem = g - g1.astype(F32)
        g2 = rem.astype(BF16)
        g3 = (rem - g2.astype(F32)).astype(BF16)
        bcum_ref[rows, :] = _dot(tri, g1) + _dot(tri, g2) + _dot(tri, g3)


def _norm_qk(x, norm_w, wgk, wup, b, wt, n_qk, tm=512):
    m, d = x.shape
    r = wgk.shape[0]
    n = wup.shape[1]
    row_block = lambda width: pl.BlockSpec((tm, width), lambda i: (i, 0))
    const_block = lambda shape: pl.BlockSpec(shape, lambda i: (0, 0))
    return pl.pallas_call(
        functools.partial(_norm_qk_kernel, tm=tm),
        out_shape=(jax.ShapeDtypeStruct((m, d), BF16),
                   jax.ShapeDtypeStruct((m, n), F32),
                   jax.ShapeDtypeStruct((m, n_qk), BF16)),
        grid=(m // tm,),
        in_specs=[row_block(d), const_block((1, d)), const_block((r, d)),
                  const_block((r, n)), const_block((1, n)),
                  pl.BlockSpec((pl.Element(n_qk), pl.Element(d)), lambda i: (0, 0),
                               pipeline_mode=pl.Buffered(1))],
        out_specs=(row_block(d), row_block(n), row_block(n_qk)),
        scratch_shapes=[pltpu.VMEM((d, n_qk), BF16)],
        compiler_params=_params(("arbitrary",)),
        name="norm_qk",
    )(x, norm_w.reshape(1, d), wgk, wup, b, wt)


def _weight_spec(d, tn, col0):
    assert col0 % SUBLANES == 0
    return pl.BlockSpec((pl.Element(tn), pl.Element(d)),
                        lambda j, i: (pl.multiple_of(col0 + j * tn, SUBLANES), 0))


def _load_weight(wt_ref):
    return wt_ref[...].T.astype(BF16)


def _row_subtiles(tm):
    r = 0
    while tm - r > ROW_SUB:
        yield r, ROW_SUB
        r += ROW_SUB
    while r < tm:
        yield r, ROW_SUB_LAST
        r += ROW_SUB_LAST


def _hn_spec(tm, d):
    return pl.BlockSpec((tm, d), lambda j, i: (i, 0))


def _out_spec(tm, tn):
    return pl.BlockSpec((tm, tn), lambda j, i: (i, j))


def _plain_kernel(h_ref, wt_ref, o_ref, wbf_ref, *, tm):
    @pl.when(pl.program_id(1) == 0)
    def _():
        wbf_ref[...] = _load_weight(wt_ref)

    for r in range(0, tm, 2 * ROW_SUB):
        rows = slice(r, r + 2 * ROW_SUB)
        o_ref[rows, :] = _dot(h_ref[rows, :], wbf_ref[...]).astype(o_ref.dtype)


def _in_plain(hn, wt, col0, n, out_dtype, tm=2048, tn=1024):
    m, d = hn.shape
    return pl.pallas_call(
        functools.partial(_plain_kernel, tm=tm),
        out_shape=jax.ShapeDtypeStruct((m, n), out_dtype),
        grid=(n // tn, m // tm),
        in_specs=[_hn_spec(tm, d), _weight_spec(d, tn, col0)],
        out_specs=_out_spec(tm, tn),
        scratch_shapes=[pltpu.VMEM((d, tn), BF16)],
        compiler_params=_params(("arbitrary", "arbitrary")),
        name="in_plain",
    )(hn, wt)


def _gate_a_kernel(h_ref, gt_ref, mt_ref, o_ref, wg_ref, wm_ref, *, tm):
    @pl.when(pl.program_id(1) == 0)
    def _():
        wg_ref[...] = _load_weight(gt_ref)
        wm_ref[...] = _load_weight(mt_ref)

    for r, n in _row_subtiles(tm):
        h = h_ref[r:r + n, :]
        g = _dot(h, wg_ref[...])
        m = _dot(h, wm_ref[...])
        o_ref[r:r + n, :] = (g * _sigmoid(g) * _sigmoid(m)).astype(o_ref.dtype)


def _in_gate_a(hn, wt, g0, m0, n, tm=2048, tn=512):
    m, d = hn.shape
    return pl.pallas_call(
        functools.partial(_gate_a_kernel, tm=tm),
        out_shape=jax.ShapeDtypeStruct((m, n), BF16),
        grid=(n // tn, m // tm),
        in_specs=[_hn_spec(tm, d), _weight_spec(d, tn, g0), _weight_spec(d, tn, m0)],
        out_specs=_out_spec(tm, tn),
        scratch_shapes=[pltpu.VMEM((d, tn), BF16)] * 2,
        compiler_params=_params(("arbitrary", "arbitrary")),
        name="in_gate_a",
    )(hn, wt, wt)


def _conv_kernel(*refs, tm, tiles_per_seq, n_cast):
    h_ref, bt_ref, ct_ref, xt_ref, mt_ref, cw_ref = refs[:6]
    cast_in = refs[6:6 + n_cast]
    o_ref = refs[6 + n_cast]
    cast_out = refs[7 + n_cast:7 + 2 * n_cast]
    wb_ref, wc_ref, wx_ref, wm_ref, work_ref, halo_ref = refs[7 + 2 * n_cast:]
    i = pl.program_id(1)

    @pl.when(i == 0)
    def _():
        wb_ref[...] = _load_weight(bt_ref)
        wc_ref[...] = _load_weight(ct_ref)
        wx_ref[...] = _load_weight(xt_ref)
        wm_ref[...] = _load_weight(mt_ref)

    @pl.when(i % tiles_per_seq == 0)
    def _():
        work_ref[0:SUBLANES, :] = jnp.zeros((SUBLANES, work_ref.shape[1]), F32)

    @pl.when(i % tiles_per_seq != 0)
    def _():
        work_ref[0:SUBLANES, :] = halo_ref[...]

    for src, dst in zip(cast_in, cast_out):
        dst[...] = src[...].astype(BF16)

    cw = cw_ref[...]
    for r, n in _row_subtiles(tm):
        h = h_ref[r:r + n, :]
        u = _dot(h, wc_ref[...]) * _dot(h, wx_ref[...])
        work_ref[SUBLANES + r:SUBLANES + r + n, :] = u
        u1 = work_ref[SUBLANES - 1 + r:SUBLANES - 1 + r + n, :]
        u2 = work_ref[SUBLANES - 2 + r:SUBLANES - 2 + r + n, :]
        conv = cw[0:1, :] * u2 + cw[1:2, :] * u1 + cw[2:3, :] * u
        y_b = _dot(h, wb_ref[...]) * conv
        o_ref[r:r + n, :] = (
            _sigmoid(_dot(h, wm_ref[...])) * y_b).astype(o_ref.dtype)
    halo_ref[...] = work_ref[tm:tm + SUBLANES, :]


def _cast_slab_spec(w, max_steps, n_row_tiles):
    steps = max(s for s in range(1, max_steps + 1)
                if w.shape[0] % s == 0 and (w.shape[0] // s) % (2 * SUBLANES) == 0)
    return pl.BlockSpec(
        (w.shape[0] // steps, w.shape[1]),
        lambda j, i: (jnp.minimum(j * n_row_tiles + i, steps - 1), 0))


def _in_conv(hn, wt, b0, c0, x0, m0, n, conv_w, seq, cast_weights, tm=2048, tn=256):
    m, d = hn.shape
    grid = (n // tn, m // tm)
    cast_specs = [_cast_slab_spec(w, grid[0] * grid[1], grid[1]) for w in cast_weights]
    kern = functools.partial(_conv_kernel, tm=tm, tiles_per_seq=seq // tm,
                             n_cast=len(cast_weights))
    return pl.pallas_call(
        kern,
        out_shape=[jax.ShapeDtypeStruct((m, n), BF16)]
                  + [jax.ShapeDtypeStruct(w.shape, BF16) for w in cast_weights],
        grid=grid,
        in_specs=[_hn_spec(tm, d)]
                 + [_weight_spec(d, tn, c) for c in (b0, c0, x0, m0)]
                 + [pl.BlockSpec((SUBLANES, tn), lambda j, i: (0, j))]
                 + cast_specs,
        out_specs=[_out_spec(tm, tn)] + cast_specs,
        scratch_shapes=[pltpu.VMEM((d, tn), BF16)] * 4
                       + [pltpu.VMEM((tm + SUBLANES, tn), F32),
                          pltpu.VMEM((SUBLANES, tn), F32)],
        compiler_params=_params(("arbitrary", "arbitrary")),
        name="in_conv",
    )(hn, wt, wt, wt, wt, conv_w, *cast_weights)


def _gla_kernel(q_ref, k_ref, v_ref, b_ref, ga_ref, yb_ref, nw_ref, x_ref, wo_ref,
                o_ref, state_ref, merged_ref, *, rows, dk, dv):
    @pl.when(pl.program_id(1) == 0)
    def _():
        state_ref[...] = jnp.zeros(state_ref.shape, F32)

    row = lax.broadcasted_iota(jnp.int32, (CHUNK, PAIR), 0)
    col = lax.broadcasted_iota(jnp.int32, (CHUNK, PAIR), 1)
    mask_a = (row >= col)[:, :CHUNK]
    mask_b = col <= row + CHUNK
    scale = dk ** -0.5
    nw = nw_ref[...]
    nt = (((1,), (1,)), ((), ()))
    tn = (((0,), (0,)), ((), ()))

    def pair_body(c):
        rs = slice(c * PAIR, (c + 1) * PAIR)
        heads = range(GLA_HEADS)
        s_as, s_bs, q_ints, k_sts, decay_cols = [], [], [], [], []
        for h in heads:
            ks = slice(h * dk, (h + 1) * dk)
            b = b_ref[rs, ks]
            b_last_a = b[CHUNK - 1:CHUNK, :]
            b_last_b = b[PAIR - 1:PAIR, :]
            b_end = jnp.concatenate([jnp.broadcast_to(b_last_a, (CHUNK, dk)),
                                     jnp.broadcast_to(b_last_b, (CHUNK, dk))], axis=0)
            q = q_ref[rs, ks].astype(F32)
            k = k_ref[rs, ks].astype(F32)
            qd = q * (scale * jnp.exp(b))
            ke = k * jnp.exp(b_end - b)
            q_dec = qd.astype(BF16)
            k_dec = (k * jnp.exp(-b)).astype(BF16)
            k_end = ke.astype(BF16)
            q_ints.append(jnp.concatenate(
                [q_dec[:CHUNK], (qd[CHUNK:] * jnp.exp(b_last_a)).astype(BF16)], axis=0))
            k_sts.append(jnp.concatenate(
                [(ke[:CHUNK] * jnp.exp(b_last_b)).astype(BF16), k_end[CHUNK:]], axis=0))
            k_x = jnp.concatenate([k_end[:CHUNK], k_dec[CHUNK:]], axis=0)
            s_as.append(lax.dot_general(q_dec[:CHUNK], k_dec[:CHUNK], nt,
                                        preferred_element_type=F32))
            s_bs.append(lax.dot_general(q_dec[CHUNK:], k_x, nt,
                                        preferred_element_type=F32))
            decay_cols.append(jnp.transpose(
                jnp.broadcast_to(jnp.exp(b_last_a + b_last_b), (LANES, dk))))
        for h in heads:
            vs = slice(h * dv, (h + 1) * dv)
            v = v_ref[rs, vs]
            o_inter = _dot(q_ints[h], state_ref[h].astype(BF16))
            kv = lax.dot_general(k_sts[h], v, tn, preferred_element_type=F32)
            p_a = jnp.where(mask_a, s_as[h], 0.0).astype(BF16)
            p_b = jnp.where(mask_b, s_bs[h], 0.0).astype(BF16)
            o = jnp.concatenate([_dot(p_a, v[:CHUNK]), _dot(p_b, v)], axis=0) + o_inter
            decay = jnp.concatenate([decay_cols[h]] * (dv // LANES), axis=1)
            state_ref[h] = state_ref[h] * decay + kv
            y = o * lax.rsqrt(jnp.mean(o * o, axis=-1, keepdims=True) + EPS) * nw
            merged_ref[rs, vs] = y.astype(BF16) * ga_ref[rs, vs] + yb_ref[rs, vs]

    for g in range(rows // OUT_ROWS):
        for c in range(g * OUT_ROWS // PAIR, (g + 1) * OUT_ROWS // PAIR):
            pair_body(c)
        r = slice(g * OUT_ROWS, (g + 1) * OUT_ROWS)
        o_ref[r, :] = x_ref[r, :] + _dot(merged_ref[r, :], wo_ref[...])


def _gla_out(qk, v, bcum, gate_a, y_b, norm_w, x, w_out, batch, seq, rows=512):
    m, vd = v.shape
    d = x.shape[1]
    assert w_out.shape == (vd, d)
    kd = bcum.shape[1]
    dk = kd // GLA_HEADS
    dv = vd // GLA_HEADS
    nblk = seq // rows
    assert qk.shape[1] == 2 * kd
    row_map = lambda b, s: (b * nblk + s, 0)
    kern = functools.partial(_gla_kernel, rows=rows, dk=dk, dv=dv)
    return pl.pallas_call(
        kern,
        out_shape=jax.ShapeDtypeStruct((m, d), F32),
        grid=(batch, nblk),
        in_specs=[pl.BlockSpec((rows, kd), lambda b, s: (b * nblk + s, 0)),
                  pl.BlockSpec((rows, kd), lambda b, s: (b * nblk + s, 1)),
                  pl.BlockSpec((rows, vd), row_map),
                  pl.BlockSpec((rows, kd), row_map),
                  pl.BlockSpec((rows, vd), row_map),
                  pl.BlockSpec((rows, vd), row_map),
                  pl.BlockSpec((1, dv), lambda b, s: (0, 0)),
                  pl.BlockSpec((rows, d), row_map),
                  pl.BlockSpec((vd, d), lambda b, s: (0, 0),
                               pipeline_mode=pl.Buffered(1))],
        out_specs=pl.BlockSpec((rows, d), row_map),
        scratch_shapes=[pltpu.VMEM((GLA_HEADS, dk, dv), F32),
                        pltpu.VMEM((rows, vd), BF16)],
        compiler_params=_params(("arbitrary", "arbitrary")),
        name="gla_out",
    )(qk, qk, v, bcum, gate_a, y_b, norm_w, x, w_out)


def _ffn_kernel(h_ref, nw_ref, wg_ref, wu_ref, wd_ref, fw_ref, o_ref, hn_ref):
    f = pl.program_id(1)

    @pl.when(f == 0)
    def _():
        h = h_ref[...]
        y = h * lax.rsqrt(jnp.mean(h * h, axis=-1, keepdims=True) + EPS)
        hn_ref[...] = (y * nw_ref[...]).astype(BF16)
        o_ref[...] = h

    hn = hn_ref[...]
    gate = _dot(hn, wg_ref[...])
    up = _dot(hn, wu_ref[...])
    act = (gate * _sigmoid(gate) * up).astype(BF16)
    o_ref[...] += _dot(act, wd_ref[...])

    @pl.when(f == pl.num_programs(1) - 1)
    def _():
        z = o_ref[...]
        y = z * lax.rsqrt(jnp.mean(z * z, axis=-1, keepdims=True) + EPS)
        o_ref[...] = y * fw_ref[...]


def _ffn(h1, norm_w, w_gate_up, wd, final_w, tm=1024, tf=512):
    m, d = h1.shape
    hidden = wd.shape[0]
    nf = hidden // tf
    row = pl.BlockSpec((tm, d), lambda i, f: (i, 0))
    vec = pl.BlockSpec((1, d), lambda i, f: (0, 0))
    return pl.pallas_call(
        _ffn_kernel,
        out_shape=jax.ShapeDtypeStruct((m, d), F32),
        grid=(m // tm, nf),
        in_specs=[row, vec,
                  pl.BlockSpec((d, tf), lambda i, f: (0, f)),
                  pl.BlockSpec((d, tf), lambda i, f: (0, nf + f)),
                  pl.BlockSpec((tf, d), lambda i, f: (f, 0)),
                  vec],
        out_specs=row,
        scratch_shapes=[pltpu.VMEM((tm, d), BF16)],
        compiler_params=_params(("arbitrary", "arbitrary")),
        name="ffn",
    )(h1, norm_w.reshape(1, d), w_gate_up, w_gate_up, wd, final_w.reshape(1, d))


def _mixer_layer(x2, batch, seq, mix_norm_w, w_in, w_gk_up, b_gk_up, gla_norm_w,
                 conv_w, w_out, ffn_norm_w, w_gate_up, w_down, final_w):
    d = x2.shape[1]
    kd = w_gk_up.shape[1]
    vd = d

    o_go = 2 * kd + vd
    o_gk = o_go + vd
    o_b = o_gk + GK_RANK
    o_c, o_x, o_ma, o_mb = o_b + d, o_b + 2 * d, o_b + 3 * d, o_b + 4 * d
    wt = jnp.swapaxes(w_in, 0, 1)
    w_gk = jnp.pad(wt[o_gk:o_b].astype(BF16), ((0, LANES - GK_RANK), (0, 0)))
    w_up = jnp.pad(w_gk_up.astype(BF16), ((0, LANES - GK_RANK), (0, 0)))
    conv_w8 = jnp.pad(conv_w, ((0, SUBLANES - CONV_K), (0, 0)))

    hn, bcum, qk = _norm_qk(x2, mix_norm_w, w_gk, w_up, b_gk_up.reshape(1, kd),
                            wt, 2 * kd)
    v = _in_plain(hn, wt, 2 * kd, vd, BF16)
    gate_a = _in_gate_a(hn, wt, o_go, o_ma, vd)
    y_b, w_out16, w_gate_up16, w_down16 = _in_conv(
        hn, wt, o_b, o_c, o_x, o_mb, d, conv_w8, seq, [w_out, w_gate_up, w_down])
    h1 = _gla_out(qk, v, bcum, gate_a, y_b, gla_norm_w.reshape(1, -1), x2, w_out16,
                  batch, seq)
    return _ffn(h1, ffn_norm_w, w_gate_up16, w_down16, final_w)


def kernel(x, mix_norm_w, w_in, w_gk_up, b_gk_up, gla_norm_w, conv_w, w_out,
           ffn_norm_w, w_gate_up, w_down, final_norm_w):
    batch, seq, d = x.shape
    assert mix_norm_w.shape[0] == 1, "single-layer block"
    out = _mixer_layer(x.reshape(batch * seq, d), batch, seq, mix_norm_w[0],
                       w_in[0], w_gk_up[0], b_gk_up[0], gla_norm_w[0], conv_w[0],
                       w_out[0], ffn_norm_w[0], w_gate_up[0], w_down[0],
                       final_norm_w)
    return out.reshape(batch, seq, d)
```

```python
import functools

import jax
import jax.numpy as jnp
from jax import lax
from jax.experimental import pallas as pl
from jax.experimental.pallas import tpu as pltpu

EPS = 1e-6
GLA_HEADS = 4
GK_RANK = 16
GATE_LOGIT_NORMALIZER = 16.0
CHUNK = 64
PAIR = 2 * CHUNK
CUM_ROWS = PAIR
OUT_ROWS = 2 * PAIR
CONV_K = 3
ROW_SUB = 256
ROW_SUB_LAST = 128

LANES = 128
SUBLANES = 8
VMEM_LIMIT = 56 * 1024 * 1024

F32 = jnp.float32
BF16 = jnp.bfloat16


def _params(semantics):
    return pltpu.CompilerParams(dimension_semantics=semantics,
                                vmem_limit_bytes=VMEM_LIMIT)


def _sigmoid(x):
    return 1.0 / (1.0 + jnp.exp(-x))


def _dot(a, b):
    return jnp.dot(a, b, preferred_element_type=F32)


def _norm_qk_kernel(x_ref, nw_ref, wgk_ref, wup_ref, b_ref, wt_ref,
                    hn_ref, bcum_ref, qk_ref, wbf_ref, *, tm):
    @pl.when(pl.program_id(0) == 0)
    def _():
        wbf_ref[...] = wt_ref[...].T.astype(BF16)

    row = lax.broadcasted_iota(jnp.int32, (CUM_ROWS, CUM_ROWS), 0)
    col = lax.broadcasted_iota(jnp.int32, (CUM_ROWS, CUM_ROWS), 1)
    tri = ((row >= col) & (row // CHUNK == col // CHUNK)).astype(BF16)
    subtiles = range(0, tm, CUM_ROWS)
    lows = []
    for r in subtiles:
        rows = slice(r, r + CUM_ROWS)
        x = x_ref[rows, :]
        y = x * lax.rsqrt(jnp.mean(x * x, axis=-1, keepdims=True) + EPS)
        hn = (y * nw_ref[...]).astype(BF16)
        hn_ref[rows, :] = hn
        lows.append(lax.dot_general(hn, wgk_ref[...], (((1,), (1,)), ((), ())),
                                    preferred_element_type=F32).astype(BF16))
    gks = [_dot(low, wup_ref[...]) + b_ref[...] for low in lows]
    for r, gk in zip(subtiles, gks):
        rows = slice(r, r + CUM_ROWS)
        qk_ref[rows, :] = _dot(hn_ref[rows, :], wbf_ref[...]).astype(qk_ref.dtype)
        log_sig = jnp.minimum(gk, 0.0) - jnp.log(1.0 + jnp.exp(-jnp.abs(gk)))
        g = log_sig * (1.0 / GATE_LOGIT_NORMALIZER)
        g1 = g.astype(BF16)
        rem = g - g1.astype(F32)
        g2 = rem.astype(BF16)
        g3 = (rem - g2.astype(F32)).astype(BF16)
        bcum_ref[rows, :] = _dot(tri, g1) + _dot(tri, g2) + _dot(tri, g3)


def _norm_qk(x, norm_w, wgk, wup, b, wt, n_qk, tm=512):
    m, d = x.shape
    r = wgk.shape[0]
    n = wup.shape[1]
    row_block = lambda width: pl.BlockSpec((tm, width), lambda i: (i, 0))
    const_block = lambda shape: pl.BlockSpec(shape, lambda i: (0, 0))
    return pl.pallas_call(
        functools.partial(_norm_qk_kernel, tm=tm),
        out_shape=(jax.ShapeDtypeStruct((m, d), BF16),
                   jax.ShapeDtypeStruct((m, n), F32),
                   jax.ShapeDtypeStruct((m, n_qk), BF16)),
        grid=(m // tm,),
        in_specs=[row_block(d), const_block((1, d)), const_block((r, d)),
                  const_block((r, n)), const_block((1, n)),
                  pl.BlockSpec((pl.Element(n_qk), pl.Element(d)), lambda i: (0, 0),
                               pipeline_mode=pl.Buffered(1))],
        out_specs=(row_block(d), row_block(n), row_block(n_qk)),
        scratch_shapes=[pltpu.VMEM((d, n_qk), BF16)],
        compiler_params=_params(("arbitrary",)),
        name="norm_qk",
    )(x, norm_w.reshape(1, d), wgk, wup, b, wt)


def _weight_spec(d, tn, col0):
    assert col0 % SUBLANES == 0
    return pl.BlockSpec((pl.Element(tn), pl.Element(d)),
                        lambda j, i: (pl.multiple_of(col0 + j * tn, SUBLANES), 0))


def _load_weight(wt_ref):
    return wt_ref[...].T.astype(BF16)


def _row_subtiles(tm, sub=ROW_SUB):
    r = 0
    while tm - r > sub:
        yield r, sub
        r += sub
    while r < tm:
        yield r, min(sub, ROW_SUB_LAST)
        r += min(sub, ROW_SUB_LAST)


def _hn_spec(tm, d):
    return pl.BlockSpec((tm, d), lambda j, i: (i, 0))


def _out_spec(tm, tn):
    return pl.BlockSpec((tm, tn), lambda j, i: (i, j))


def _plain_kernel(h_ref, wt_ref, o_ref, wbf_ref, *, tm):
    @pl.when(pl.program_id(1) == 0)
    def _():
        wbf_ref[...] = _load_weight(wt_ref)

    for r in range(0, tm, 2 * ROW_SUB):
        rows = slice(r, r + 2 * ROW_SUB)
        o_ref[rows, :] = _dot(h_ref[rows, :], wbf_ref[...]).astype(o_ref.dtype)


def _in_plain(hn, wt, col0, n, out_dtype, tm=2048, tn=1024):
    m, d = hn.shape
    return pl.pallas_call(
        functools.partial(_plain_kernel, tm=tm),
        out_shape=jax.ShapeDtypeStruct((m, n), out_dtype),
        grid=(n // tn, m // tm),
        in_specs=[_hn_spec(tm, d), _weight_spec(d, tn, col0)],
        out_specs=_out_spec(tm, tn),
        scratch_shapes=[pltpu.VMEM((d, tn), BF16)],
        compiler_params=_params(("arbitrary", "arbitrary")),
        name="in_plain",
    )(hn, wt)


def _gate_a_kernel(h_ref, gt_ref, mt_ref, o_ref, wg_ref, wm_ref, *, tm):
    @pl.when(pl.program_id(1) == 0)
    def _():
        wg_ref[...] = _load_weight(gt_ref)
        wm_ref[...] = _load_weight(mt_ref)

    for r, n in _row_subtiles(tm, ROW_SUB_LAST):
        h = h_ref[r:r + n, :]
        g = _dot(h, wg_ref[...])
        m = _dot(h, wm_ref[...])
        o_ref[r:r + n, :] = (g * _sigmoid(g) * _sigmoid(m)).astype(o_ref.dtype)


def _in_gate_a(hn, wt, g0, m0, n, tm=2048, tn=512):
    m, d = hn.shape
    return pl.pallas_call(
        functools.partial(_gate_a_kernel, tm=tm),
        out_shape=jax.ShapeDtypeStruct((m, n), BF16),
        grid=(n // tn, m // tm),
        in_specs=[_hn_spec(tm, d), _weight_spec(d, tn, g0), _weight_spec(d, tn, m0)],
        out_specs=_out_spec(tm, tn),
        scratch_shapes=[pltpu.VMEM((d, tn), BF16)] * 2,
        compiler_params=_params(("arbitrary", "arbitrary")),
        name="in_gate_a",
    )(hn, wt, wt)


def _conv_kernel(*refs, tm, tiles_per_seq, n_cast):
    h_ref, bt_ref, ct_ref, xt_ref, mt_ref, cw_ref = refs[:6]
    cast_in = refs[6:6 + n_cast]
    o_ref = refs[6 + n_cast]
    cast_out = refs[7 + n_cast:7 + 2 * n_cast]
    wb_ref, wc_ref, wx_ref, wm_ref, work_ref, halo_ref = refs[7 + 2 * n_cast:]
    i = pl.program_id(1)

    @pl.when(i == 0)
    def _():
        wb_ref[...] = _load_weight(bt_ref)
        wc_ref[...] = _load_weight(ct_ref)
        wx_ref[...] = _load_weight(xt_ref)
        wm_ref[...] = _load_weight(mt_ref)

    @pl.when(i % tiles_per_seq == 0)
    def _():
        work_ref[0:SUBLANES, :] = jnp.zeros((SUBLANES, work_ref.shape[1]), F32)

    @pl.when(i % tiles_per_seq != 0)
    def _():
        work_ref[0:SUBLANES, :] = halo_ref[...]

    for src, dst in zip(cast_in, cast_out):
        dst[...] = src[...].astype(BF16)

    cw = cw_ref[...]
    for r, n in _row_subtiles(tm):
        h = h_ref[r:r + n, :]
        u = _dot(h, wc_ref[...]) * _dot(h, wx_ref[...])
        work_ref[SUBLANES + r:SUBLANES + r + n, :] = u
        u1 = work_ref[SUBLANES - 1 + r:SUBLANES - 1 + r + n, :]
        u2 = work_ref[SUBLANES - 2 + r:SUBLANES - 2 + r + n, :]
        conv = cw[0:1, :] * u2 + cw[1:2, :] * u1 + cw[2:3, :] * u
        y_b = _dot(h, wb_ref[...]) * conv
        o_ref[r:r + n, :] = (
            _sigmoid(_dot(h, wm_ref[...])) * y_b).astype(o_ref.dtype)
    halo_ref[...] = work_ref[tm:tm + SUBLANES, :]


def _cast_slab_spec(w, max_steps, n_row_tiles):
    steps = max(s for s in range(1, max_steps + 1)
                if w.shape[0] % s == 0 and (w.shape[0] // s) % (2 * SUBLANES) == 0)
    return pl.BlockSpec(
        (w.shape[0] // steps, w.shape[1]),
        lambda j, i: (jnp.minimum(j * n_row_tiles + i, steps - 1), 0))


def _in_conv(hn, wt, b0, c0, x0, m0, n, conv_w, seq, cast_weights, tm=2048, tn=256):
    m, d = hn.shape
    grid = (n // tn, m // tm)
    cast_specs = [_cast_slab_spec(w, grid[0] * grid[1], grid[1]) for w in cast_weights]
    kern = functools.partial(_conv_kernel, tm=tm, tiles_per_seq=seq // tm,
                             n_cast=len(cast_weights))
    return pl.pallas_call(
        kern,
        out_shape=[jax.ShapeDtypeStruct((m, n), BF16)]
                  + [jax.ShapeDtypeStruct(w.shape, BF16) for w in cast_weights],
        grid=grid,
        in_specs=[_hn_spec(tm, d)]
                 + [_weight_spec(d, tn, c) for c in (b0, c0, x0, m0)]
                 + [pl.BlockSpec((SUBLANES, tn), lambda j, i: (0, j))]
                 + cast_specs,
        out_specs=[_out_spec(tm, tn)] + cast_specs,
        scratch_shapes=[pltpu.VMEM((d, tn), BF16)] * 4
                       + [pltpu.VMEM((tm + SUBLANES, tn), F32),
                          pltpu.VMEM((SUBLANES, tn), F32)],
        compiler_params=_params(("arbitrary", "arbitrary")),
        name="in_conv",
    )(hn, wt, wt, wt, wt, conv_w, *cast_weights)


def _gla_kernel(q_ref, k_ref, v_ref, b_ref, ga_ref, yb_ref, nw_ref, x_ref, wo_ref,
                o_ref, state_ref, merged_ref, *, rows, dk, dv):
    @pl.when(pl.program_id(1) == 0)
    def _():
        state_ref[...] = jnp.zeros(state_ref.shape, F32)

    row = lax.broadcasted_iota(jnp.int32, (CHUNK, PAIR), 0)
    col = lax.broadcasted_iota(jnp.int32, (CHUNK, PAIR), 1)
    mask_a = (row >= col)[:, :CHUNK]
    mask_b = col <= row + CHUNK
    scale = dk ** -0.5
    nw = nw_ref[...]
    nt = (((1,), (1,)), ((), ()))
    tn = (((0,), (0,)), ((), ()))

    def pair_body(c):
        rs = slice(c * PAIR, (c + 1) * PAIR)

        def scores_stage(h):
            ks = slice(h * dk, (h + 1) * dk)
            b = b_ref[rs, ks]
            b_last_a = b[CHUNK - 1:CHUNK, :]
            b_last_b = b[PAIR - 1:PAIR, :]
            b_end = jnp.concatenate([jnp.broadcast_to(b_last_a, (CHUNK, dk)),
                                     jnp.broadcast_to(b_last_b, (CHUNK, dk))], axis=0)
            q = q_ref[rs, ks].astype(F32)
            k = k_ref[rs, ks].astype(F32)
            qd = q * (scale * jnp.exp(b))
            ke = k * jnp.exp(b_end - b)
            q_dec = qd.astype(BF16)
            k_dec = (k * jnp.exp(-b)).astype(BF16)
            k_end = ke.astype(BF16)
            q_int = jnp.concatenate(
                [q_dec[:CHUNK], (qd[CHUNK:] * jnp.exp(b_last_a)).astype(BF16)], axis=0)
            k_st = jnp.concatenate(
                [(ke[:CHUNK] * jnp.exp(b_last_b)).astype(BF16), k_end[CHUNK:]], axis=0)
            k_x = jnp.concatenate([k_end[:CHUNK], k_dec[CHUNK:]], axis=0)
            s_a = lax.dot_general(q_dec[:CHUNK], k_dec[:CHUNK], nt,
                                  preferred_element_type=F32)
            s_b = lax.dot_general(q_dec[CHUNK:], k_x, nt, preferred_element_type=F32)
            decay_col = jnp.transpose(
                jnp.broadcast_to(jnp.exp(b_last_a + b_last_b), (LANES, dk)))
            return s_a, s_b, q_int, k_st, decay_col

        def update_stage(h, s_a, s_b, q_int, k_st, decay_col):
            vs = slice(h * dv, (h + 1) * dv)
            v = v_ref[rs, vs]
            o_inter = _dot(q_int, state_ref[h].astype(BF16))
            kv = lax.dot_general(k_st, v, tn, preferred_element_type=F32)
            p_a = jnp.where(mask_a, s_a, 0.0).astype(BF16)
            p_b = jnp.where(mask_b, s_b, 0.0).astype(BF16)
            o = jnp.concatenate([_dot(p_a, v[:CHUNK]), _dot(p_b, v)], axis=0) + o_inter
            decay = jnp.concatenate([decay_col] * (dv // LANES), axis=1)
            state_ref[h] = state_ref[h] * decay + kv
            y = o * lax.rsqrt(jnp.mean(o * o, axis=-1, keepdims=True) + EPS) * nw
            merged_ref[rs, vs] = y.astype(BF16) * ga_ref[rs, vs] + yb_ref[rs, vs]

        staged = scores_stage(0)
        for h in range(GLA_HEADS):
            ahead = scores_stage(h + 1) if h + 1 < GLA_HEADS else None
            update_stage(h, *staged)
            staged = ahead

    for g in range(rows // OUT_ROWS):
        for c in range(g * OUT_ROWS // PAIR, (g + 1) * OUT_ROWS // PAIR):
            pair_body(c)
        r = slice(g * OUT_ROWS, (g + 1) * OUT_ROWS)
        o_ref[r, :] = x_ref[r, :] + _dot(merged_ref[r, :], wo_ref[...])


def _gla_out(qk, v, bcum, gate_a, y_b, norm_w, x, w_out, batch, seq, rows=512):
    m, vd = v.shape
    d = x.shape[1]
    assert w_out.shape == (vd, d)
    kd = bcum.shape[1]
    dk = kd // GLA_HEADS
    dv = vd // GLA_HEADS
    nblk = seq // rows
    assert qk.shape[1] == 2 * kd
    row_map = lambda b, s: (b * nblk + s, 0)
    kern = functools.partial(_gla_kernel, rows=rows, dk=dk, dv=dv)
    return pl.pallas_call(
        kern,
        out_shape=jax.ShapeDtypeStruct((m, d), F32),
        grid=(batch, nblk),
        in_specs=[pl.BlockSpec((rows, kd), lambda b, s: (b * nblk + s, 0)),
                  pl.BlockSpec((rows, kd), lambda b, s: (b * nblk + s, 1)),
                  pl.BlockSpec((rows, vd), row_map),
                  pl.BlockSpec((rows, kd), row_map),
                  pl.BlockSpec((rows, vd), row_map),
                  pl.BlockSpec((rows, vd), row_map),
                  pl.BlockSpec((1, dv), lambda b, s: (0, 0)),
                  pl.BlockSpec((rows, d), row_map),
                  pl.BlockSpec((vd, d), lambda b, s: (0, 0),
                               pipeline_mode=pl.Buffered(1))],
        out_specs=pl.BlockSpec((rows, d), row_map),
        scratch_shapes=[pltpu.VMEM((GLA_HEADS, dk, dv), F32),
                        pltpu.VMEM((rows, vd), BF16)],
        compiler_params=_params(("arbitrary", "arbitrary")),
        name="gla_out",
    )(qk, qk, v, bcum, gate_a, y_b, norm_w, x, w_out)


def _ffn_kernel(h_ref, nw_ref, wg_ref, wu_ref, wd_ref, fw_ref, o_ref, hn_ref):
    f = pl.program_id(1)

    @pl.when(f == 0)
    def _():
        h = h_ref[...]
        y = h * lax.rsqrt(jnp.mean(h * h, axis=-1, keepdims=True) + EPS)
        hn_ref[...] = (y * nw_ref[...]).astype(BF16)
        o_ref[...] = h

    hn = hn_ref[...]
    gate = _dot(hn, wg_ref[...])
    up = _dot(hn, wu_ref[...])
    act = (gate * _sigmoid(gate) * up).astype(BF16)
    o_ref[...] += _dot(act, wd_ref[...])

    @pl.when(f == pl.num_programs(1) - 1)
    def _():
        z = o_ref[...]
        y = z * lax.rsqrt(jnp.mean(z * z, axis=-1, keepdims=True) + EPS)
        o_ref[...] = y * fw_ref[...]


def _ffn(h1, norm_w, w_gate_up, wd, final_w, tm=1024, tf=512):
    m, d = h1.shape
    hidden = wd.shape[0]
    nf = hidden // tf
    row = pl.BlockSpec((tm, d), lambda i, f: (i, 0))
    vec = pl.BlockSpec((1, d), lambda i, f: (0, 0))
    return pl.pallas_call(
        _ffn_kernel,
        out_shape=jax.ShapeDtypeStruct((m, d), F32),
        grid=(m // tm, nf),
        in_specs=[row, vec,
                  pl.BlockSpec((d, tf), lambda i, f: (0, f)),
                  pl.BlockSpec((d, tf), lambda i, f: (0, nf + f)),
                  pl.BlockSpec((tf, d), lambda i, f: (f, 0)),
                  vec],
        out_specs=row,
        scratch_shapes=[pltpu.VMEM((tm, d), BF16)],
        compiler_params=_params(("arbitrary", "arbitrary")),
        name="ffn",
    )(h1, norm_w.reshape(1, d), w_gate_up, w_gate_up, wd, final_w.reshape(1, d))


def _mixer_layer(x2, batch, seq, mix_norm_w, w_in, w_gk_up, b_gk_up, gla_norm_w,
                 conv_w, w_out, ffn_norm_w, w_gate_up, w_down, final_w):
    d = x2.shape[1]
    kd = w_gk_up.shape[1]
    vd = d

    o_go = 2 * kd + vd
    o_gk = o_go + vd
    o_b = o_gk + GK_RANK
    o_c, o_x, o_ma, o_mb = o_b + d, o_b + 2 * d, o_b + 3 * d, o_b + 4 * d
    wt = jnp.swapaxes(w_in, 0, 1)
    w_gk = jnp.pad(wt[o_gk:o_b].astype(BF16), ((0, LANES - GK_RANK), (0, 0)))
    w_up = jnp.pad(w_gk_up.astype(BF16), ((0, LANES - GK_RANK), (0, 0)))
    conv_w8 = jnp.pad(conv_w, ((0, SUBLANES - CONV_K), (0, 0)))

    hn, bcum, qk = _norm_qk(x2, mix_norm_w, w_gk, w_up, b_gk_up.reshape(1, kd),
                            wt, 2 * kd)
    v = _in_plain(hn, wt, 2 * kd, vd, BF16)
    gate_a = _in_gate_a(hn, wt, o_go, o_ma, vd)
    y_b, w_out16, w_gate_up16, w_down16 = _in_conv(
        hn, wt, o_b, o_c, o_x, o_mb, d, conv_w8, seq, [w_out, w_gate_up, w_down])
    h1 = _gla_out(qk, v, bcum, gate_a, y_b, gla_norm_w.reshape(1, -1), x2, w_out16,
                  batch, seq)
    return _ffn(h1, ffn_norm_w, w_gate_up16, w_down16, final_w)


def kernel(x, mix_norm_w, w_in, w_gk_up, b_gk_up, gla_norm_w, conv_w, w_out,
           ffn_norm_w, w_gate_up, w_down, final_norm_w):
    batch, seq, d = x.shape
    assert mix_norm_w.shape[0] == 1, "single-layer block"
    out = _mixer_layer(x.reshape(batch * seq, d), batch, seq, mix_norm_w[0],
                       w_in[0], w_gk_up[0], b_gk_up[0], gla_norm_w[0], conv_w[0],
                       w_out[0], ffn_norm_w[0], w_gate_up[0], w_down[0],
                       final_norm_w)
    return out.reshape(batch, seq, d)
```

```python
import functools

import jax
import jax.numpy as jnp
from jax import lax
from jax.experimental import pallas as pl
from jax.experimental.pallas import tpu as pltpu

EPS = 1e-6
GLA_HEADS = 4
GK_RANK = 16
GATE_LOGIT_NORMALIZER = 16.0
CHUNK = 64
PAIR = 2 * CHUNK
CUM_ROWS = PAIR
OUT_ROWS = 2 * PAIR
CONV_K = 3
ROW_SUB = 256
ROW_SUB_LAST = 128

LANES = 128
SUBLANES = 8
VMEM_LIMIT = 56 * 1024 * 1024

F32 = jnp.float32
BF16 = jnp.bfloat16


def _params(semantics):
    return pltpu.CompilerParams(dimension_semantics=semantics,
                                vmem_limit_bytes=VMEM_LIMIT)


def _sigmoid(x):
    return 1.0 / (1.0 + jnp.exp(-x))


def _dot(a, b):
    return jnp.dot(a, b, preferred_element_type=F32)


def _norm_qk_kernel(x_ref, nw_ref, wgk_ref, wup_ref, b_ref, wt_ref,
                    hn_ref, bcum_ref, qk_ref, wbf_ref, *, tm):
    @pl.when(pl.program_id(0) == 0)
    def _():
        wbf_ref[...] = wt_ref[...].T.astype(BF16)

    row = lax.broadcasted_iota(jnp.int32, (CUM_ROWS, CUM_ROWS), 0)
    col = lax.broadcasted_iota(jnp.int32, (CUM_ROWS, CUM_ROWS), 1)
    tri = ((row >= col) & (row // CHUNK == col // CHUNK)).astype(BF16)
    subtiles = range(0, tm, CUM_ROWS)
    lows = []
    for r in subtiles:
        rows = slice(r, r + CUM_ROWS)
        x = x_ref[rows, :]
        y = x * lax.rsqrt(jnp.mean(x * x, axis=-1, keepdims=True) + EPS)
        hn = (y * nw_ref[...]).astype(BF16)
        hn_ref[rows, :] = hn
        lows.append(lax.dot_general(hn, wgk_ref[...], (((1,), (1,)), ((), ())),
                                    preferred_element_type=F32).astype(BF16))
    gks = [_dot(low, wup_ref[...]) + b_ref[...] for low in lows]
    for r, gk in zip(subtiles, gks):
        rows = slice(r, r + CUM_ROWS)
        qk_ref[rows, :] = _dot(hn_ref[rows, :], wbf_ref[...]).astype(qk_ref.dtype)
        log_sig = jnp.minimum(gk, 0.0) - jnp.log(1.0 + jnp.exp(-jnp.abs(gk)))
        g = log_sig * (1.0 / GATE_LOGIT_NORMALIZER)
        g1 = g.astype(BF16)
        rem = g - g1.astype(F32)
        g2 = rem.astype(BF16)
        g3 = (rem - g2.astype(F32)).astype(BF16)
        bcum_ref[rows, :] = _dot(tri, g1) + _dot(tri, g2) + _dot(tri, g3)


def _norm_qk(x, norm_w, wgk, wup, b, wt, n_qk, tm=512):
    m, d = x.shape
    r = wgk.shape[0]
    n = wup.shape[1]
    row_block = lambda width: pl.BlockSpec((tm, width), lambda i: (i, 0))
    const_block = lambda shape: pl.BlockSpec(shape, lambda i: (0, 0))
    return pl.pallas_call(
        functools.partial(_norm_qk_kernel, tm=tm),
        out_shape=(jax.ShapeDtypeStruct((m, d), BF16),
                   jax.ShapeDtypeStruct((m, n), F32),
                   jax.ShapeDtypeStruct((m, n_qk), BF16)),
        grid=(m // tm,),
        in_specs=[row_block(d), const_block((1, d)), const_block((r, d)),
                  const_block((r, n)), const_block((1, n)),
                  pl.BlockSpec((pl.Element(n_qk), pl.Element(d)), lambda i: (0, 0),
                               pipeline_mode=pl.Buffered(1))],
        out_specs=(row_block(d), row_block(n), row_block(n_qk)),
        scratch_shapes=[pltpu.VMEM((d, n_qk), BF16)],
        compiler_params=_params(("arbitrary",)),
        name="norm_qk",
    )(x, norm_w.reshape(1, d), wgk, wup, b, wt)


def _weight_spec(d, tn, col0):
    assert col0 % SUBLANES == 0
    return pl.BlockSpec((pl.Element(tn), pl.Element(d)),
                        lambda j, i: (pl.multiple_of(col0 + j * tn, SUBLANES), 0))


def _load_weight(wt_ref):
    return wt_ref[...].T.astype(BF16)


def _row_subtiles(tm, sub=ROW_SUB):
    r = 0
    while tm - r > sub:
        yield r, sub
        r += sub
    while r < tm:
        yield r, min(sub, ROW_SUB_LAST)
        r += min(sub, ROW_SUB_LAST)


def _hn_spec(tm, d):
    return pl.BlockSpec((tm, d), lambda j, i: (i, 0))


def _out_spec(tm, tn):
    return pl.BlockSpec((tm, tn), lambda j, i: (i, j))


def _plain_kernel(h_ref, wt_ref, o_ref, wbf_ref, *, tm):
    @pl.when(pl.program_id(1) == 0)
    def _():
        wbf_ref[...] = _load_weight(wt_ref)

    for r in range(0, tm, 2 * ROW_SUB):
        rows = slice(r, r + 2 * ROW_SUB)
        o_ref[rows, :] = _dot(h_ref[rows, :], wbf_ref[...]).astype(o_ref.dtype)


def _in_plain(hn, wt, col0, n, out_dtype, tm=2048, tn=1024):
    m, d = hn.shape
    return pl.pallas_call(
        functools.partial(_plain_kernel, tm=tm),
        out_shape=jax.ShapeDtypeStruct((m, n), out_dtype),
        grid=(n // tn, m // tm),
        in_specs=[_hn_spec(tm, d), _weight_spec(d, tn, col0)],
        out_specs=_out_spec(tm, tn),
        scratch_shapes=[pltpu.VMEM((d, tn), BF16)],
        compiler_params=_params(("arbitrary", "arbitrary")),
        name="in_plain",
    )(hn, wt)


def _gate_a_kernel(h_ref, gt_ref, mt_ref, o_ref, wg_ref, wm_ref, *, tm):
    @pl.when(pl.program_id(1) == 0)
    def _():
        wg_ref[...] = _load_weight(gt_ref)
        wm_ref[...] = _load_weight(mt_ref)

    for r, n in _row_subtiles(tm, ROW_SUB_LAST):
        h = h_ref[r:r + n, :]
        g = _dot(h, wg_ref[...])
        m = _dot(h, wm_ref[...])
        o_ref[r:r + n, :] = (g * _sigmoid(g) * _sigmoid(m)).astype(o_ref.dtype)


def _in_gate_a(hn, wt, g0, m0, n, tm=2048, tn=512):
    m, d = hn.shape
    return pl.pallas_call(
        functools.partial(_gate_a_kernel, tm=tm),
        out_shape=jax.ShapeDtypeStruct((m, n), BF16),
        grid=(n // tn, m // tm),
        in_specs=[_hn_spec(tm, d), _weight_spec(d, tn, g0), _weight_spec(d, tn, m0)],
        out_specs=_out_spec(tm, tn),
        scratch_shapes=[pltpu.VMEM((d, tn), BF16)] * 2,
        compiler_params=_params(("arbitrary", "arbitrary")),
        name="in_gate_a",
    )(hn, wt, wt)


def _conv_kernel(*refs, tm, tiles_per_seq, n_cast):
    h_ref, bt_ref, ct_ref, xt_ref, mt_ref, cw_ref = refs[:6]
    cast_in = refs[6:6 + n_cast]
    o_ref = refs[6 + n_cast]
    cast_out = refs[7 + n_cast:7 + 2 * n_cast]
    wb_ref, wc_ref, wx_ref, wm_ref, work_ref, halo_ref = refs[7 + 2 * n_cast:]
    i = pl.program_id(1)

    @pl.when(i == 0)
    def _():
        wb_ref[...] = _load_weight(bt_ref)
        wc_ref[...] = _load_weight(ct_ref)
        wx_ref[...] = _load_weight(xt_ref)
        wm_ref[...] = _load_weight(mt_ref)

    @pl.when(i % tiles_per_seq == 0)
    def _():
        work_ref[0:SUBLANES, :] = jnp.zeros((SUBLANES, work_ref.shape[1]), F32)

    @pl.when(i % tiles_per_seq != 0)
    def _():
        work_ref[0:SUBLANES, :] = halo_ref[...]

    for src, dst in zip(cast_in, cast_out):
        dst[...] = src[...].astype(BF16)

    cw = cw_ref[...]
    for r, n in _row_subtiles(tm):
        h = h_ref[r:r + n, :]
        u = _dot(h, wc_ref[...]) * _dot(h, wx_ref[...])
        work_ref[SUBLANES + r:SUBLANES + r + n, :] = u
        u1 = work_ref[SUBLANES - 1 + r:SUBLANES - 1 + r + n, :]
        u2 = work_ref[SUBLANES - 2 + r:SUBLANES - 2 + r + n, :]
        conv = cw[0:1, :] * u2 + cw[1:2, :] * u1 + cw[2:3, :] * u
        y_b = _dot(h, wb_ref[...]) * conv
        o_ref[r:r + n, :] = (
            _sigmoid(_dot(h, wm_ref[...])) * y_b).astype(o_ref.dtype)
    halo_ref[...] = work_ref[tm:tm + SUBLANES, :]


def _cast_slab_spec(w, max_steps, n_row_tiles):
    steps = max(s for s in range(1, max_steps + 1)
                if w.shape[0] % s == 0 and (w.shape[0] // s) % (2 * SUBLANES) == 0)
    return pl.BlockSpec(
        (w.shape[0] // steps, w.shape[1]),
        lambda j, i: (jnp.minimum(j * n_row_tiles + i, steps - 1), 0))


def _in_conv(hn, wt, b0, c0, x0, m0, n, conv_w, seq, cast_weights, tm=2048, tn=256):
    m, d = hn.shape
    grid = (n // tn, m // tm)
    cast_specs = [_cast_slab_spec(w, grid[0] * grid[1], grid[1]) for w in cast_weights]
    kern = functools.partial(_conv_kernel, tm=tm, tiles_per_seq=seq // tm,
                             n_cast=len(cast_weights))
    return pl.pallas_call(
        kern,
        out_shape=[jax.ShapeDtypeStruct((m, n), BF16)]
                  + [jax.ShapeDtypeStruct(w.shape, BF16) for w in cast_weights],
        grid=grid,
        in_specs=[_hn_spec(tm, d)]
                 + [_weight_spec(d, tn, c) for c in (b0, c0, x0, m0)]
                 + [pl.BlockSpec((SUBLANES, tn), lambda j, i: (0, j))]
                 + cast_specs,
        out_specs=[_out_spec(tm, tn)] + cast_specs,
        scratch_shapes=[pltpu.VMEM((d, tn), BF16)] * 4
                       + [pltpu.VMEM((tm + SUBLANES, tn), F32),
                          pltpu.VMEM((SUBLANES, tn), F32)],
        compiler_params=_params(("arbitrary", "arbitrary")),
        name="in_conv",
    )(hn, wt, wt, wt, wt, conv_w, *cast_weights)


def _gla_kernel(q_ref, k_ref, v_ref, b_ref, ga_ref, yb_ref, nw_ref, x_ref, wo_ref,
                o_ref, state_ref, merged_ref, *, rows, dk, dv):
    @pl.when(pl.program_id(1) == 0)
    def _():
        state_ref[...] = jnp.zeros(state_ref.shape, F32)

    row = lax.broadcasted_iota(jnp.int32, (CHUNK, PAIR), 0)
    col = lax.broadcasted_iota(jnp.int32, (CHUNK, PAIR), 1)
    mask_a = (row >= col)[:, :CHUNK]
    mask_b = col <= row + CHUNK
    scale = dk ** -0.5
    nw = nw_ref[...]
    nt = (((1,), (1,)), ((), ()))
    tn = (((0,), (0,)), ((), ()))

    def pair_body(c):
        rs = slice(c * PAIR, (c + 1) * PAIR)

        def scores_stage(h):
            ks = slice(h * dk, (h + 1) * dk)
            b = b_ref[rs, ks]
            b_last_a = b[CHUNK - 1:CHUNK, :]
            b_last_b = b[PAIR - 1:PAIR, :]
            b_end = jnp.concatenate([jnp.broadcast_to(b_last_a, (CHUNK, dk)),
                                     jnp.broadcast_to(b_last_b, (CHUNK, dk))], axis=0)
            q = q_ref[rs, ks].astype(F32)
            k = k_ref[rs, ks].astype(F32)
            qd = q * (scale * jnp.exp(b))
            ke = k * jnp.exp(b_end - b)
            q_dec = qd.astype(BF16)
            k_dec = (k * jnp.exp(-b)).astype(BF16)
            k_end = ke.astype(BF16)
            q_int = jnp.concatenate(
                [q_dec[:CHUNK], (qd[CHUNK:] * jnp.exp(b_last_a)).astype(BF16)], axis=0)
            k_st = jnp.concatenate(
                [(ke[:CHUNK] * jnp.exp(b_last_b)).astype(BF16), k_end[CHUNK:]], axis=0)
            k_x = jnp.concatenate([k_end[:CHUNK], k_dec[CHUNK:]], axis=0)
            s_a = lax.dot_general(q_dec[:CHUNK], k_dec[:CHUNK], nt,
                                  preferred_element_type=F32)
            s_b = lax.dot_general(q_dec[CHUNK:], k_x, nt, preferred_element_type=F32)
            decay_col = jnp.transpose(
                jnp.broadcast_to(jnp.exp(b_last_a + b_last_b), (LANES, dk)))
            return s_a, s_b, q_int, k_st, decay_col

        def update_stage(h, s_a, s_b, q_int, k_st, decay_col):
            vs = slice(h * dv, (h + 1) * dv)
            v = v_ref[rs, vs]
            kv = lax.dot_general(k_st, v, tn, preferred_element_type=F32)
            state = state_ref[h]
            o_inter = _dot(q_int, state.astype(BF16))
            decay = jnp.concatenate([decay_col] * (dv // LANES), axis=1)
            state_ref[h] = state * decay + kv
            p_a = jnp.where(mask_a, s_a, 0.0).astype(BF16)
            p_b = jnp.where(mask_b, s_b, 0.0).astype(BF16)
            o = jnp.concatenate([_dot(p_a, v[:CHUNK]), _dot(p_b, v)], axis=0) + o_inter
            y = o * lax.rsqrt(jnp.mean(o * o, axis=-1, keepdims=True) + EPS) * nw
            merged_ref[rs, vs] = y.astype(BF16) * ga_ref[rs, vs] + yb_ref[rs, vs]

        staged = scores_stage(0)
        for h in range(GLA_HEADS):
            ahead = scores_stage(h + 1) if h + 1 < GLA_HEADS else None
            update_stage(h, *staged)
            staged = ahead

    for g in range(rows // OUT_ROWS):
        for c in range(g * OUT_ROWS // PAIR, (g + 1) * OUT_ROWS // PAIR):
            pair_body(c)
        r = slice(g * OUT_ROWS, (g + 1) * OUT_ROWS)
        o_ref[r, :] = x_ref[r, :] + _dot(merged_ref[r, :], wo_ref[...])


def _gla_out(qk, v, bcum, gate_a, y_b, norm_w, x, w_out, batch, seq, rows=512):
    m, vd = v.shape
    d = x.shape[1]
    assert w_out.shape == (vd, d)
    kd = bcum.shape[1]
    dk = kd // GLA_HEADS
    dv = vd // GLA_HEADS
    nblk = seq // rows
    assert qk.shape[1] == 2 * kd
    row_map = lambda b, s: (b * nblk + s, 0)
    kern = functools.partial(_gla_kernel, rows=rows, dk=dk, dv=dv)
    return pl.pallas_call(
        kern,
        out_shape=jax.ShapeDtypeStruct((m, d), F32),
        grid=(batch, nblk),
        in_specs=[pl.BlockSpec((rows, kd), lambda b, s: (b * nblk + s, 0)),
                  pl.BlockSpec((rows, kd), lambda b, s: (b * nblk + s, 1)),
                  pl.BlockSpec((rows, vd), row_map),
                  pl.BlockSpec((rows, kd), row_map),
                  pl.BlockSpec((rows, vd), row_map),
                  pl.BlockSpec((rows, vd), row_map),
                  pl.BlockSpec((1, dv), lambda b, s: (0, 0)),
                  pl.BlockSpec((rows, d), row_map),
                  pl.BlockSpec((vd, d), lambda b, s: (0, 0),
                               pipeline_mode=pl.Buffered(1))],
        out_specs=pl.BlockSpec((rows, d), row_map),
        scratch_shapes=[pltpu.VMEM((GLA_HEADS, dk, dv), F32),
                        pltpu.VMEM((rows, vd), BF16)],
        compiler_params=_params(("arbitrary", "arbitrary")),
        name="gla_out",
    )(qk, qk, v, bcum, gate_a, y_b, norm_w, x, w_out)


def _ffn_kernel(h_ref, nw_ref, wg_ref, wu_ref, wd_ref, fw_ref, o_ref, hn_ref):
    f = pl.program_id(1)

    @pl.when(f == 0)
    def _():
        h = h_ref[...]
        y = h * lax.rsqrt(jnp.mean(h * h, axis=-1, keepdims=True) + EPS)
        hn_ref[...] = (y * nw_ref[...]).astype(BF16)
        o_ref[...] = h

    hn = hn_ref[...]
    gate = _dot(hn, wg_ref[...])
    up = _dot(hn, wu_ref[...])
    act = (gate * _sigmoid(gate) * up).astype(BF16)
    o_ref[...] += _dot(act, wd_ref[...])

    @pl.when(f == pl.num_programs(1) - 1)
    def _():
        z = o_ref[...]
        y = z * lax.rsqrt(jnp.mean(z * z, axis=-1, keepdims=True) + EPS)
        o_ref[...] = y * fw_ref[...]


def _ffn(h1, norm_w, w_gate_up, wd, final_w, tm=1024, tf=512):
    m, d = h1.shape
    hidden = wd.shape[0]
    nf = hidden // tf
    row = pl.BlockSpec((tm, d), lambda i, f: (i, 0))
    vec = pl.BlockSpec((1, d), lambda i, f: (0, 0))
    return pl.pallas_call(
        _ffn_kernel,
        out_shape=jax.ShapeDtypeStruct((m, d), F32),
        grid=(m // tm, nf),
        in_specs=[row, vec,
                  pl.BlockSpec((d, tf), lambda i, f: (0, f)),
                  pl.BlockSpec((d, tf), lambda i, f: (0, nf + f)),
                  pl.BlockSpec((tf, d), lambda i, f: (f, 0)),
                  vec],
        out_specs=row,
        scratch_shapes=[pltpu.VMEM((tm, d), BF16)],
        compiler_params=_params(("arbitrary", "arbitrary")),
        name="ffn",
    )(h1, norm_w.reshape(1, d), w_gate_up, w_gate_up, wd, final_w.reshape(1, d))


def _mixer_layer(x2, batch, seq, mix_norm_w, w_in, w_gk_up, b_gk_up, gla_norm_w,
                 conv_w, w_out, ffn_norm_w, w_gate_up, w_down, final_w):
    d = x2.shape[1]
    kd = w_gk_up.shape[1]
    vd = d

    o_go = 2 * kd + vd
    o_gk = o_go + vd
    o_b = o_gk + GK_RANK
    o_c, o_x, o_ma, o_mb = o_b + d, o_b + 2 * d, o_b + 3 * d, o_b + 4 * d
    wt = jnp.swapaxes(w_in, 0, 1)
    w_gk = jnp.pad(wt[o_gk:o_b].astype(BF16), ((0, LANES - GK_RANK), (0, 0)))
    w_up = jnp.pad(w_gk_up.astype(BF16), ((0, LANES - GK_RANK), (0, 0)))
    conv_w8 = jnp.pad(conv_w, ((0, SUBLANES - CONV_K), (0, 0)))

    hn, bcum, qk = _norm_qk(x2, mix_norm_w, w_gk, w_up, b_gk_up.reshape(1, kd),
                            wt, 2 * kd)
    v = _in_plain(hn, wt, 2 * kd, vd, BF16)
    gate_a = _in_gate_a(hn, wt, o_go, o_ma, vd)
    y_b, w_out16, w_gate_up16, w_down16 = _in_conv(
        hn, wt, o_b, o_c, o_x, o_mb, d, conv_w8, seq, [w_out, w_gate_up, w_down])
    h1 = _gla_out(qk, v, bcum, gate_a, y_b, gla_norm_w.reshape(1, -1), x2, w_out16,
                  batch, seq)
    return _ffn(h1, ffn_norm_w, w_gate_up16, w_down16, final_w)


def kernel(x, mix_norm_w, w_in, w_gk_up, b_gk_up, gla_norm_w, conv_w, w_out,
           ffn_norm_w, w_gate_up, w_down, final_norm_w):
    batch, seq, d = x.shape
    assert mix_norm_w.shape[0] == 1, "single-layer block"
    out = _mixer_layer(x.reshape(batch * seq, d), batch, seq, mix_norm_w[0],
                       w_in[0], w_gk_up[0], b_gk_up[0], gla_norm_w[0], conv_w[0],
                       w_out[0], ffn_norm_w[0], w_gate_up[0], w_down[0],
                       final_norm_w)
    return out.reshape(batch, seq, d)
```

```python
import functools

import jax
import jax.numpy as jnp
from jax import lax
from jax.experimental import pallas as pl
from jax.experimental.pallas import tpu as pltpu

EPS = 1e-6
GLA_HEADS = 4
GK_RANK = 16
GATE_LOGIT_NORMALIZER = 16.0
CHUNK = 64
PAIR = 2 * CHUNK
CUM_ROWS = PAIR
OUT_ROWS = 2 * PAIR
CONV_K = 3
ROW_SUB = 256
ROW_SUB_LAST = 128

LANES = 128
SUBLANES = 8
VMEM_LIMIT = 56 * 1024 * 1024

F32 = jnp.float32
BF16 = jnp.bfloat16


def _params(semantics):
    return pltpu.CompilerParams(dimension_semantics=semantics,
                                vmem_limit_bytes=VMEM_LIMIT)


def _sigmoid(x):
    return 1.0 / (1.0 + jnp.exp(-x))


def _dot(a, b):
    return jnp.dot(a, b, preferred_element_type=F32)


def _norm_qk_kernel(x_ref, nw_ref, wgk_ref, wup_ref, b_ref, wt_ref,
                    hn_ref, bcum_ref, qk_ref, wbf_ref, *, tm):
    @pl.when(pl.program_id(0) == 0)
    def _():
        wbf_ref[...] = wt_ref[...].T.astype(BF16)

    row = lax.broadcasted_iota(jnp.int32, (CUM_ROWS, CUM_ROWS), 0)
    col = lax.broadcasted_iota(jnp.int32, (CUM_ROWS, CUM_ROWS), 1)
    tri = ((row >= col) & (row // CHUNK == col // CHUNK)).astype(BF16)
    subtiles = range(0, tm, CUM_ROWS)
    lows = []
    for r in subtiles:
        rows = slice(r, r + CUM_ROWS)
        x = x_ref[rows, :]
        y = x * lax.rsqrt(jnp.mean(x * x, axis=-1, keepdims=True) + EPS)
        hn = (y * nw_ref[...]).astype(BF16)
        hn_ref[rows, :] = hn
        lows.append(lax.dot_general(hn, wgk_ref[...], (((1,), (1,)), ((), ())),
                                    preferred_element_type=F32).astype(BF16))
    gks = [_dot(low, wup_ref[...]) + b_ref[...] for low in lows]
    for r, gk in zip(subtiles, gks):
        rows = slice(r, r + CUM_ROWS)
        qk_ref[rows, :] = _dot(hn_ref[rows, :], wbf_ref[...]).astype(qk_ref.dtype)
        log_sig = jnp.minimum(gk, 0.0) - jnp.log(1.0 + jnp.exp(-jnp.abs(gk)))
        g = log_sig * (1.0 / GATE_LOGIT_NORMALIZER)
        g1 = g.astype(BF16)
        rem = g - g1.astype(F32)
        g2 = rem.astype(BF16)
        g3 = (rem - g2.astype(F32)).astype(BF16)
        bcum_ref[rows, :] = _dot(tri, g1) + _dot(tri, g2) + _dot(tri, g3)


def _norm_qk(x, norm_w, wgk, wup, b, wt, n_qk, tm=512):
    m, d = x.shape
    r = wgk.shape[0]
    n = wup.shape[1]
    row_block = lambda width: pl.BlockSpec((tm, width), lambda i: (i, 0))
    const_block = lambda shape: pl.BlockSpec(shape, lambda i: (0, 0))
    return pl.pallas_call(
        functools.partial(_norm_qk_kernel, tm=tm),
        out_shape=(jax.ShapeDtypeStruct((m, d), BF16),
                   jax.ShapeDtypeStruct((m, n), F32),
                   jax.ShapeDtypeStruct((m, n_qk), BF16)),
        grid=(m // tm,),
        in_specs=[row_block(d), const_block((1, d)), const_block((r, d)),
                  const_block((r, n)), const_block((1, n)),
                  pl.BlockSpec((pl.Element(n_qk), pl.Element(d)), lambda i: (0, 0),
                               pipeline_mode=pl.Buffered(1))],
        out_specs=(row_block(d), row_block(n), row_block(n_qk)),
        scratch_shapes=[pltpu.VMEM((d, n_qk), BF16)],
        compiler_params=_params(("arbitrary",)),
        name="norm_qk",
    )(x, norm_w.reshape(1, d), wgk, wup, b, wt)


def _weight_spec(d, tn, col0):
    assert col0 % SUBLANES == 0
    return pl.BlockSpec((pl.Element(tn), pl.Element(d)),
                        lambda j, i: (pl.multiple_of(col0 + j * tn, SUBLANES), 0))


def _load_weight(wt_ref):
    return wt_ref[...].T.astype(BF16)


def _row_subtiles(tm, sub=ROW_SUB):
    r = 0
    while tm - r > sub:
        yield r, sub
        r += sub
    while r < tm:
        yield r, min(sub, ROW_SUB_LAST)
        r += min(sub, ROW_SUB_LAST)


def _hn_spec(tm, d):
    return pl.BlockSpec((tm, d), lambda j, i: (i, 0))


def _out_spec(tm, tn):
    return pl.BlockSpec((tm, tn), lambda j, i: (i, j))


def _plain_kernel(h_ref, wt_ref, o_ref, wbf_ref, *, tm):
    @pl.when(pl.program_id(1) == 0)
    def _():
        wbf_ref[...] = _load_weight(wt_ref)

    for r in range(0, tm, 2 * ROW_SUB):
        rows = slice(r, r + 2 * ROW_SUB)
        o_ref[rows, :] = _dot(h_ref[rows, :], wbf_ref[...]).astype(o_ref.dtype)


def _in_plain(hn, wt, col0, n, out_dtype, tm=2048, tn=1024):
    m, d = hn.shape
    return pl.pallas_call(
        functools.partial(_plain_kernel, tm=tm),
        out_shape=jax.ShapeDtypeStruct((m, n), out_dtype),
        grid=(n // tn, m // tm),
        in_specs=[_hn_spec(tm, d), _weight_spec(d, tn, col0)],
        out_specs=_out_spec(tm, tn),
        scratch_shapes=[pltpu.VMEM((d, tn), BF16)],
        compiler_params=_params(("arbitrary", "arbitrary")),
        name="in_plain",
    )(hn, wt)


def _gate_a_kernel(h_ref, gt_ref, mt_ref, o_ref, wg_ref, wm_ref, *, tm):
    @pl.when(pl.program_id(1) == 0)
    def _():
        wg_ref[...] = _load_weight(gt_ref)
        wm_ref[...] = _load_weight(mt_ref)

    for r, n in _row_subtiles(tm, ROW_SUB_LAST):
        h = h_ref[r:r + n, :]
        g = _dot(h, wg_ref[...])
        m = _dot(h, wm_ref[...])
        o_ref[r:r + n, :] = (g * _sigmoid(g) * _sigmoid(m)).astype(o_ref.dtype)


def _in_gate_a(hn, wt, g0, m0, n, tm=2048, tn=512):
    m, d = hn.shape
    return pl.pallas_call(
        functools.partial(_gate_a_kernel, tm=tm),
        out_shape=jax.ShapeDtypeStruct((m, n), BF16),
        grid=(n // tn, m // tm),
        in_specs=[_hn_spec(tm, d), _weight_spec(d, tn, g0), _weight_spec(d, tn, m0)],
        out_specs=_out_spec(tm, tn),
        scratch_shapes=[pltpu.VMEM((d, tn), BF16)] * 2,
        compiler_params=_params(("arbitrary", "arbitrary")),
        name="in_gate_a",
    )(hn, wt, wt)


def _conv_kernel(*refs, tm, tiles_per_seq, n_cast):
    h_ref, bt_ref, ct_ref, xt_ref, mt_ref, cw_ref = refs[:6]
    cast_in = refs[6:6 + n_cast]
    o_ref = refs[6 + n_cast]
    cast_out = refs[7 + n_cast:7 + 2 * n_cast]
    wb_ref, wc_ref, wx_ref, wm_ref, work_ref, halo_ref = refs[7 + 2 * n_cast:]
    i = pl.program_id(1)

    @pl.when(i == 0)
    def _():
        wb_ref[...] = _load_weight(bt_ref)
        wc_ref[...] = _load_weight(ct_ref)
        wx_ref[...] = _load_weight(xt_ref)
        wm_ref[...] = _load_weight(mt_ref)

    @pl.when(i % tiles_per_seq == 0)
    def _():
        work_ref[0:SUBLANES, :] = jnp.zeros((SUBLANES, work_ref.shape[1]), F32)

    @pl.when(i % tiles_per_seq != 0)
    def _():
        work_ref[0:SUBLANES, :] = halo_ref[...]

    for src, dst in zip(cast_in, cast_out):
        dst[...] = src[...].astype(BF16)

    cw = cw_ref[...]
    for r, n in _row_subtiles(tm):
        h = h_ref[r:r + n, :]
        u = _dot(h, wc_ref[...]) * _dot(h, wx_ref[...])
        work_ref[SUBLANES + r:SUBLANES + r + n, :] = u
        u1 = work_ref[SUBLANES - 1 + r:SUBLANES - 1 + r + n, :]
        u2 = work_ref[SUBLANES - 2 + r:SUBLANES - 2 + r + n, :]
        conv = cw[0:1, :] * u2 + cw[1:2, :] * u1 + cw[2:3, :] * u
        y_b = _dot(h, wb_ref[...]) * conv
        o_ref[r:r + n, :] = (
            _sigmoid(_dot(h, wm_ref[...])) * y_b).astype(o_ref.dtype)
    halo_ref[...] = work_ref[tm:tm + SUBLANES, :]


def _cast_slab_spec(w, max_steps, n_row_tiles):
    steps = max(s for s in range(1, max_steps + 1)
                if w.shape[0] % s == 0 and (w.shape[0] // s) % (2 * SUBLANES) == 0)
    return pl.BlockSpec(
        (w.shape[0] // steps, w.shape[1]),
        lambda j, i: (jnp.minimum(j * n_row_tiles + i, steps - 1), 0))


def _in_conv(hn, wt, b0, c0, x0, m0, n, conv_w, seq, cast_weights, tm=2048, tn=256):
    m, d = hn.shape
    grid = (n // tn, m // tm)
    cast_specs = [_cast_slab_spec(w, grid[0] * grid[1], grid[1]) for w in cast_weights]
    kern = functools.partial(_conv_kernel, tm=tm, tiles_per_seq=seq // tm,
                             n_cast=len(cast_weights))
    return pl.pallas_call(
        kern,
        out_shape=[jax.ShapeDtypeStruct((m, n), BF16)]
                  + [jax.ShapeDtypeStruct(w.shape, BF16) for w in cast_weights],
        grid=grid,
        in_specs=[_hn_spec(tm, d)]
                 + [_weight_spec(d, tn, c) for c in (b0, c0, x0, m0)]
                 + [pl.BlockSpec((SUBLANES, tn), lambda j, i: (0, j))]
                 + cast_specs,
        out_specs=[_out_spec(tm, tn)] + cast_specs,
        scratch_shapes=[pltpu.VMEM((d, tn), BF16)] * 4
                       + [pltpu.VMEM((tm + SUBLANES, tn), F32),
                          pltpu.VMEM((SUBLANES, tn), F32)],
        compiler_params=_params(("arbitrary", "arbitrary")),
        name="in_conv",
    )(hn, wt, wt, wt, wt, conv_w, *cast_weights)


def _gla_kernel(q_ref, k_ref, v_ref, b_ref, ga_ref, yb_ref, nw_ref, x_ref, wo_ref,
                o_ref, state_ref, merged_ref, *, rows, dk, dv):
    @pl.when(pl.program_id(1) == 0)
    def _():
        state_ref[...] = jnp.zeros(state_ref.shape, F32)

    row = lax.broadcasted_iota(jnp.int32, (CHUNK, PAIR), 0)
    col = lax.broadcasted_iota(jnp.int32, (CHUNK, PAIR), 1)
    mask_a = (row >= col)[:, :CHUNK]
    mask_b = col <= row + CHUNK
    scale = dk ** -0.5
    nw = nw_ref[...]
    nt = (((1,), (1,)), ((), ()))
    tn = (((0,), (0,)), ((), ()))

    def pair_body(c):
        rs = slice(c * PAIR, (c + 1) * PAIR)

        def scores_stage(h):
            ks = slice(h * dk, (h + 1) * dk)
            b = b_ref[rs, ks]
            b_last_a = b[CHUNK - 1:CHUNK, :]
            b_last_b = b[PAIR - 1:PAIR, :]
            b_end = jnp.concatenate([jnp.broadcast_to(b_last_a, (CHUNK, dk)),
                                     jnp.broadcast_to(b_last_b, (CHUNK, dk))], axis=0)
            q = q_ref[rs, ks].astype(F32)
            k = k_ref[rs, ks].astype(F32)
            qd = q * (scale * jnp.exp(b))
            ke = k * jnp.exp(b_end - b)
            q_dec = qd.astype(BF16)
            k_dec = (k * jnp.exp(-b)).astype(BF16)
            k_end = ke.astype(BF16)
            q_int = jnp.concatenate(
                [q_dec[:CHUNK], (qd[CHUNK:] * jnp.exp(b_last_a)).astype(BF16)], axis=0)
            k_st = jnp.concatenate(
                [(ke[:CHUNK] * jnp.exp(b_last_b)).astype(BF16), k_end[CHUNK:]], axis=0)
            k_x = jnp.concatenate([k_end[:CHUNK], k_dec[CHUNK:]], axis=0)
            s_a = lax.dot_general(q_dec[:CHUNK], k_dec[:CHUNK], nt,
                                  preferred_element_type=F32)
            s_b = lax.dot_general(q_dec[CHUNK:], k_x, nt, preferred_element_type=F32)
            decay_col = jnp.transpose(
                jnp.broadcast_to(jnp.exp(b_last_a + b_last_b), (LANES, dk)))
            return s_a, s_b, q_int, k_st, decay_col

        def update_stage(h, s_a, s_b, q_int, k_st, decay_col):
            vs = slice(h * dv, (h + 1) * dv)
            v = v_ref[rs, vs]
            kv = lax.dot_general(k_st, v, tn, preferred_element_type=F32)
            state = state_ref[h]
            o_inter = _dot(q_int, state.astype(BF16))
            decay = jnp.concatenate([decay_col] * (dv // LANES), axis=1)
            state_ref[h] = state * decay + kv
            p_a = jnp.where(mask_a, s_a, 0.0).astype(BF16)
            p_b = jnp.where(mask_b, s_b, 0.0).astype(BF16)
            o = jnp.concatenate([_dot(p_a, v[:CHUNK]), _dot(p_b, v)], axis=0) + o_inter
            y = o * lax.rsqrt(jnp.mean(o * o, axis=-1, keepdims=True) + EPS) * nw
            merged_ref[rs, vs] = y.astype(BF16) * ga_ref[rs, vs] + yb_ref[rs, vs]

        staged = scores_stage(0)
        for h in range(GLA_HEADS):
            ahead = scores_stage(h + 1) if h + 1 < GLA_HEADS else None
            update_stage(h, *staged)
            staged = ahead

    def project(r, cs):
        o_ref[r, cs] = x_ref[r, cs] + _dot(merged_ref[r, :], wo_ref[:, cs])

    pairs_per_group = OUT_ROWS // PAIR
    out_cols = wo_ref.shape[1] // pairs_per_group
    pending = []
    for g in range(rows // OUT_ROWS):
        for c in range(g * pairs_per_group, (g + 1) * pairs_per_group):
            pair_body(c)
            if pending:
                project(*pending.pop(0))
        r = slice(g * OUT_ROWS, (g + 1) * OUT_ROWS)
        pending = [(r, slice(j * out_cols, (j + 1) * out_cols))
                   for j in range(pairs_per_group)]
    for piece in pending:
        project(*piece)


def _gla_out(qk, v, bcum, gate_a, y_b, norm_w, x, w_out, batch, seq, rows=512):
    m, vd = v.shape
    d = x.shape[1]
    assert w_out.shape == (vd, d)
    kd = bcum.shape[1]
    dk = kd // GLA_HEADS
    dv = vd // GLA_HEADS
    nblk = seq // rows
    assert qk.shape[1] == 2 * kd
    row_map = lambda b, s: (b * nblk + s, 0)
    kern = functools.partial(_gla_kernel, rows=rows, dk=dk, dv=dv)
    return pl.pallas_call(
        kern,
        out_shape=jax.ShapeDtypeStruct((m, d), F32),
        grid=(batch, nblk),
        in_specs=[pl.BlockSpec((rows, kd), lambda b, s: (b * nblk + s, 0)),
                  pl.BlockSpec((rows, kd), lambda b, s: (b * nblk + s, 1)),
                  pl.BlockSpec((rows, vd), row_map),
                  pl.BlockSpec((rows, kd), row_map),
                  pl.BlockSpec((rows, vd), row_map),
                  pl.BlockSpec((rows, vd), row_map),
                  pl.BlockSpec((1, dv), lambda b, s: (0, 0)),
                  pl.BlockSpec((rows, d), row_map),
                  pl.BlockSpec((vd, d), lambda b, s: (0, 0),
                               pipeline_mode=pl.Buffered(1))],
        out_specs=pl.BlockSpec((rows, d), row_map),
        scratch_shapes=[pltpu.VMEM((GLA_HEADS, dk, dv), F32),
                        pltpu.VMEM((rows, vd), BF16)],
        compiler_params=_params(("arbitrary", "arbitrary")),
        name="gla_out",
    )(qk, qk, v, bcum, gate_a, y_b, norm_w, x, w_out)


def _ffn_kernel(h_ref, nw_ref, wg_ref, wu_ref, wd_ref, fw_ref, o_ref, hn_ref):
    f = pl.program_id(1)

    @pl.when(f == 0)
    def _():
        h = h_ref[...]
        y = h * lax.rsqrt(jnp.mean(h * h, axis=-1, keepdims=True) + EPS)
        hn_ref[...] = (y * nw_ref[...]).astype(BF16)
        o_ref[...] = h

    hn = hn_ref[...]
    gate = _dot(hn, wg_ref[...])
    up = _dot(hn, wu_ref[...])
    act = (gate * _sigmoid(gate) * up).astype(BF16)
    o_ref[...] += _dot(act, wd_ref[...])

    @pl.when(f == pl.num_programs(1) - 1)
    def _():
        z = o_ref[...]
        y = z * lax.rsqrt(jnp.mean(z * z, axis=-1, keepdims=True) + EPS)
        o_ref[...] = y * fw_ref[...]


def _ffn(h1, norm_w, w_gate_up, wd, final_w, tm=1024, tf=512):
    m, d = h1.shape
    hidden = wd.shape[0]
    nf = hidden // tf
    row = pl.BlockSpec((tm, d), lambda i, f: (i, 0))
    vec = pl.BlockSpec((1, d), lambda i, f: (0, 0))
    return pl.pallas_call(
        _ffn_kernel,
        out_shape=jax.ShapeDtypeStruct((m, d), F32),
        grid=(m // tm, nf),
        in_specs=[row, vec,
                  pl.BlockSpec((d, tf), lambda i, f: (0, f)),
                  pl.BlockSpec((d, tf), lambda i, f: (0, nf + f)),
                  pl.BlockSpec((tf, d), lambda i, f: (f, 0)),
                  vec],
        out_specs=row,
        scratch_shapes=[pltpu.VMEM((tm, d), BF16)],
        compiler_params=_params(("arbitrary", "arbitrary")),
        name="ffn",
    )(h1, norm_w.reshape(1, d), w_gate_up, w_gate_up, wd, final_w.reshape(1, d))


def _mixer_layer(x2, batch, seq, mix_norm_w, w_in, w_gk_up, b_gk_up, gla_norm_w,
                 conv_w, w_out, ffn_norm_w, w_gate_up, w_down, final_w):
    d = x2.shape[1]
    kd = w_gk_up.shape[1]
    vd = d

    o_go = 2 * kd + vd
    o_gk = o_go + vd
    o_b = o_gk + GK_RANK
    o_c, o_x, o_ma, o_mb = o_b + d, o_b + 2 * d, o_b + 3 * d, o_b + 4 * d
    wt = jnp.swapaxes(w_in, 0, 1)
    w_gk = jnp.pad(wt[o_gk:o_b].astype(BF16), ((0, LANES - GK_RANK), (0, 0)))
    w_up = jnp.pad(w_gk_up.astype(BF16), ((0, LANES - GK_RANK), (0, 0)))
    conv_w8 = jnp.pad(conv_w, ((0, SUBLANES - CONV_K), (0, 0)))

    hn, bcum, qk = _norm_qk(x2, mix_norm_w, w_gk, w_up, b_gk_up.reshape(1, kd),
                            wt, 2 * kd)
    v = _in_plain(hn, wt, 2 * kd, vd, BF16)
    gate_a = _in_gate_a(hn, wt, o_go, o_ma, vd)
    y_b, w_out16, w_gate_up16, w_down16 = _in_conv(
        hn, wt, o_b, o_c, o_x, o_mb, d, conv_w8, seq, [w_out, w_gate_up, w_down])
    h1 = _gla_out(qk, v, bcum, gate_a, y_b, gla_norm_w.reshape(1, -1), x2, w_out16,
                  batch, seq)
    return _ffn(h1, ffn_norm_w, w_gate_up16, w_down16, final_w)


def kernel(x, mix_norm_w, w_in, w_gk_up, b_gk_up, gla_norm_w, conv_w, w_out,
           ffn_norm_w, w_gate_up, w_down, final_norm_w):
    batch, seq, d = x.shape
    assert mix_norm_w.shape[0] == 1, "single-layer block"
    out = _mixer_layer(x.reshape(batch * seq, d), batch, seq, mix_norm_w[0],
                       w_in[0], w_gk_up[0], b_gk_up[0], gla_norm_w[0], conv_w[0],
                       w_out[0], ffn_norm_w[0], w_gate_up[0], w_down[0],
                       final_norm_w)
    return out.reshape(batch, seq, d)
```

```python
import functools

import jax
import jax.numpy as jnp
from jax import lax
from jax.experimental import pallas as pl
from jax.experimental.pallas import tpu as pltpu

EPS = 1e-6
GLA_HEADS = 4
GK_RANK = 16
GATE_LOGIT_NORMALIZER = 16.0
CHUNK = 64
PAIR = 2 * CHUNK
CUM_ROWS = PAIR
OUT_ROWS = 2 * PAIR
CONV_K = 3
ROW_SUB = 256
ROW_SUB_LAST = 128

LANES = 128
SUBLANES = 8
VMEM_LIMIT = 56 * 1024 * 1024

F32 = jnp.float32
BF16 = jnp.bfloat16


def _params(semantics):
    return pltpu.CompilerParams(dimension_semantics=semantics,
                                vmem_limit_bytes=VMEM_LIMIT)


def _sigmoid(x):
    return 1.0 / (1.0 + jnp.exp(-x))


def _dot(a, b):
    return jnp.dot(a, b, preferred_element_type=F32)


def _norm_qk_kernel(x_ref, nw_ref, wgk_ref, wup_ref, b_ref, wt_ref,
                    hn_ref, bcum_ref, qk_ref, wbf_ref, *, tm):
    @pl.when(pl.program_id(0) == 0)
    def _():
        wbf_ref[...] = wt_ref[...].T.astype(BF16)

    row = lax.broadcasted_iota(jnp.int32, (CUM_ROWS, CUM_ROWS), 0)
    col = lax.broadcasted_iota(jnp.int32, (CUM_ROWS, CUM_ROWS), 1)
    tri = ((row >= col) & (row // CHUNK == col // CHUNK)).astype(BF16)
    subtiles = range(0, tm, CUM_ROWS)
    lows = []
    for r in subtiles:
        rows = slice(r, r + CUM_ROWS)
        x = x_ref[rows, :]
        y = x * lax.rsqrt(jnp.mean(x * x, axis=-1, keepdims=True) + EPS)
        hn = (y * nw_ref[...]).astype(BF16)
        hn_ref[rows, :] = hn
        lows.append(lax.dot_general(hn, wgk_ref[...], (((1,), (1,)), ((), ())),
                                    preferred_element_type=F32).astype(BF16))
    gks = [_dot(low, wup_ref[...]) + b_ref[...] for low in lows]
    for r, gk in zip(subtiles, gks):
        rows = slice(r, r + CUM_ROWS)
        if r % (2 * CUM_ROWS) == 0:
            wide = slice(r, r + 2 * CUM_ROWS)
            qk_ref[wide, :] = _dot(hn_ref[wide, :], wbf_ref[...]).astype(qk_ref.dtype)
        log_sig = jnp.minimum(gk, 0.0) - jnp.log(1.0 + jnp.exp(-jnp.abs(gk)))
        g = log_sig * (1.0 / GATE_LOGIT_NORMALIZER)
        g1 = g.astype(BF16)
        rem = g - g1.astype(F32)
        g2 = rem.astype(BF16)
        g3 = (rem - g2.astype(F32)).astype(BF16)
        bcum_ref[rows, :] = _dot(tri, g1) + _dot(tri, g2) + _dot(tri, g3)


def _norm_qk(x, norm_w, wgk, wup, b, wt, n_qk, tm=512):
    m, d = x.shape
    r = wgk.shape[0]
    n = wup.shape[1]
    row_block = lambda width: pl.BlockSpec((tm, width), lambda i: (i, 0))
    const_block = lambda shape: pl.BlockSpec(shape, lambda i: (0, 0))
    return pl.pallas_call(
        functools.partial(_norm_qk_kernel, tm=tm),
        out_shape=(jax.ShapeDtypeStruct((m, d), BF16),
                   jax.ShapeDtypeStruct((m, n), F32),
                   jax.ShapeDtypeStruct((m, n_qk), BF16)),
        grid=(m // tm,),
        in_specs=[row_block(d), const_block((1, d)), const_block((r, d)),
                  const_block((r, n)), const_block((1, n)),
                  pl.BlockSpec((pl.Element(n_qk), pl.Element(d)), lambda i: (0, 0),
                               pipeline_mode=pl.Buffered(1))],
        out_specs=(row_block(d), row_block(n), row_block(n_qk)),
        scratch_shapes=[pltpu.VMEM((d, n_qk), BF16)],
        compiler_params=_params(("arbitrary",)),
        name="norm_qk",
    )(x, norm_w.reshape(1, d), wgk, wup, b, wt)


def _weight_spec(d, tn, col0):
    assert col0 % SUBLANES == 0
    return pl.BlockSpec((pl.Element(tn), pl.Element(d)),
                        lambda j, i: (pl.multiple_of(col0 + j * tn, SUBLANES), 0))


def _load_weight(wt_ref):
    return wt_ref[...].T.astype(BF16)


def _row_subtiles(tm, sub=ROW_SUB):
    r = 0
    while tm - r > sub:
        yield r, sub
        r += sub
    while r < tm:
        yield r, min(sub, ROW_SUB_LAST)
        r += min(sub, ROW_SUB_LAST)


def _hn_spec(tm, d):
    return pl.BlockSpec((tm, d), lambda j, i: (i, 0))


def _out_spec(tm, tn):
    return pl.BlockSpec((tm, tn), lambda j, i: (i, j))


def _plain_kernel(h_ref, wt_ref, o_ref, wbf_ref, *, tm):
    @pl.when(pl.program_id(1) == 0)
    def _():
        wbf_ref[...] = _load_weight(wt_ref)

    for r in range(0, tm, 2 * ROW_SUB):
        rows = slice(r, r + 2 * ROW_SUB)
        o_ref[rows, :] = _dot(h_ref[rows, :], wbf_ref[...]).astype(o_ref.dtype)


def _in_plain(hn, wt, col0, n, out_dtype, tm=2048, tn=1024):
    m, d = hn.shape
    return pl.pallas_call(
        functools.partial(_plain_kernel, tm=tm),
        out_shape=jax.ShapeDtypeStruct((m, n), out_dtype),
        grid=(n // tn, m // tm),
        in_specs=[_hn_spec(tm, d), _weight_spec(d, tn, col0)],
        out_specs=_out_spec(tm, tn),
        scratch_shapes=[pltpu.VMEM((d, tn), BF16)],
        compiler_params=_params(("arbitrary", "arbitrary")),
        name="in_plain",
    )(hn, wt)


def _gate_a_kernel(h_ref, gt_ref, mt_ref, o_ref, wg_ref, wm_ref, *, tm):
    @pl.when(pl.program_id(1) == 0)
    def _():
        wg_ref[...] = _load_weight(gt_ref)
        wm_ref[...] = _load_weight(mt_ref)

    for r, n in _row_subtiles(tm, ROW_SUB_LAST):
        h = h_ref[r:r + n, :]
        g = _dot(h, wg_ref[...])
        m = _dot(h, wm_ref[...])
        o_ref[r:r + n, :] = (g * _sigmoid(g) * _sigmoid(m)).astype(o_ref.dtype)


def _in_gate_a(hn, wt, g0, m0, n, tm=2048, tn=512):
    m, d = hn.shape
    return pl.pallas_call(
        functools.partial(_gate_a_kernel, tm=tm),
        out_shape=jax.ShapeDtypeStruct((m, n), BF16),
        grid=(n // tn, m // tm),
        in_specs=[_hn_spec(tm, d), _weight_spec(d, tn, g0), _weight_spec(d, tn, m0)],
        out_specs=_out_spec(tm, tn),
        scratch_shapes=[pltpu.VMEM((d, tn), BF16)] * 2,
        compiler_params=_params(("arbitrary", "arbitrary")),
        name="in_gate_a",
    )(hn, wt, wt)


def _conv_kernel(*refs, tm, tiles_per_seq, n_cast):
    h_ref, bt_ref, ct_ref, xt_ref, mt_ref, cw_ref = refs[:6]
    cast_in = refs[6:6 + n_cast]
    o_ref = refs[6 + n_cast]
    cast_out = refs[7 + n_cast:7 + 2 * n_cast]
    wb_ref, wc_ref, wx_ref, wm_ref, work_ref, halo_ref = refs[7 + 2 * n_cast:]
    i = pl.program_id(1)

    @pl.when(i == 0)
    def _():
        wb_ref[...] = _load_weight(bt_ref)
        wc_ref[...] = _load_weight(ct_ref)
        wx_ref[...] = _load_weight(xt_ref)
        wm_ref[...] = _load_weight(mt_ref)

    @pl.when(i % tiles_per_seq == 0)
    def _():
        work_ref[0:SUBLANES, :] = jnp.zeros((SUBLANES, work_ref.shape[1]), F32)

    @pl.when(i % tiles_per_seq != 0)
    def _():
        work_ref[0:SUBLANES, :] = halo_ref[...]

    for src, dst in zip(cast_in, cast_out):
        dst[...] = src[...].astype(BF16)

    cw = cw_ref[...]
    for r, n in _row_subtiles(tm):
        h = h_ref[r:r + n, :]
        u = _dot(h, wc_ref[...]) * _dot(h, wx_ref[...])
        work_ref[SUBLANES + r:SUBLANES + r + n, :] = u
        u1 = work_ref[SUBLANES - 1 + r:SUBLANES - 1 + r + n, :]
        u2 = work_ref[SUBLANES - 2 + r:SUBLANES - 2 + r + n, :]
        conv = cw[0:1, :] * u2 + cw[1:2, :] * u1 + cw[2:3, :] * u
        y_b = _dot(h, wb_ref[...]) * conv
        o_ref[r:r + n, :] = (
            _sigmoid(_dot(h, wm_ref[...])) * y_b).astype(o_ref.dtype)
    halo_ref[...] = work_ref[tm:tm + SUBLANES, :]


def _cast_slab_spec(w, max_steps, n_row_tiles):
    steps = max(s for s in range(1, max_steps + 1)
                if w.shape[0] % s == 0 and (w.shape[0] // s) % (2 * SUBLANES) == 0)
    return pl.BlockSpec(
        (w.shape[0] // steps, w.shape[1]),
        lambda j, i: (jnp.minimum(j * n_row_tiles + i, steps - 1), 0))


def _in_conv(hn, wt, b0, c0, x0, m0, n, conv_w, seq, cast_weights, tm=2048, tn=256):
    m, d = hn.shape
    grid = (n // tn, m // tm)
    cast_specs = [_cast_slab_spec(w, grid[0] * grid[1], grid[1]) for w in cast_weights]
    kern = functools.partial(_conv_kernel, tm=tm, tiles_per_seq=seq // tm,
                             n_cast=len(cast_weights))
    return pl.pallas_call(
        kern,
        out_shape=[jax.ShapeDtypeStruct((m, n), BF16)]
                  + [jax.ShapeDtypeStruct(w.shape, BF16) for w in cast_weights],
        grid=grid,
        in_specs=[_hn_spec(tm, d)]
                 + [_weight_spec(d, tn, c) for c in (b0, c0, x0, m0)]
                 + [pl.BlockSpec((SUBLANES, tn), lambda j, i: (0, j))]
                 + cast_specs,
        out_specs=[_out_spec(tm, tn)] + cast_specs,
        scratch_shapes=[pltpu.VMEM((d, tn), BF16)] * 4
                       + [pltpu.VMEM((tm + SUBLANES, tn), F32),
                          pltpu.VMEM((SUBLANES, tn), F32)],
        compiler_params=_params(("arbitrary", "arbitrary")),
        name="in_conv",
    )(hn, wt, wt, wt, wt, conv_w, *cast_weights)


def _gla_kernel(q_ref, k_ref, v_ref, b_ref, ga_ref, yb_ref, nw_ref, x_ref, wo_ref,
                o_ref, state_ref, merged_ref, *, rows, dk, dv):
    @pl.when(pl.program_id(1) == 0)
    def _():
        state_ref[...] = jnp.zeros(state_ref.shape, F32)

    row = lax.broadcasted_iota(jnp.int32, (CHUNK, PAIR), 0)
    col = lax.broadcasted_iota(jnp.int32, (CHUNK, PAIR), 1)
    mask_a = (row >= col)[:, :CHUNK]
    mask_b = col <= row + CHUNK
    scale = dk ** -0.5
    nw = nw_ref[...]
    nt = (((1,), (1,)), ((), ()))
    tn = (((0,), (0,)), ((), ()))

    def pair_body(c):
        rs = slice(c * PAIR, (c + 1) * PAIR)

        def scores_stage(h):
            ks = slice(h * dk, (h + 1) * dk)
            b = b_ref[rs, ks]
            b_last_a = b[CHUNK - 1:CHUNK, :]
            b_last_b = b[PAIR - 1:PAIR, :]
            b_end = jnp.concatenate([jnp.broadcast_to(b_last_a, (CHUNK, dk)),
                                     jnp.broadcast_to(b_last_b, (CHUNK, dk))], axis=0)
            q = q_ref[rs, ks].astype(F32)
            k = k_ref[rs, ks].astype(F32)
            qd = q * (scale * jnp.exp(b))
            ke = k * jnp.exp(b_end - b)
            q_dec = qd.astype(BF16)
            k_dec = (k * jnp.exp(-b)).astype(BF16)
            k_end = ke.astype(BF16)
            q_int = jnp.concatenate(
                [q_dec[:CHUNK], (qd[CHUNK:] * jnp.exp(b_last_a)).astype(BF16)], axis=0)
            k_st = jnp.concatenate(
                [(ke[:CHUNK] * jnp.exp(b_last_b)).astype(BF16), k_end[CHUNK:]], axis=0)
            k_x = jnp.concatenate([k_end[:CHUNK], k_dec[CHUNK:]], axis=0)
            s_a = lax.dot_general(q_dec[:CHUNK], k_dec[:CHUNK], nt,
                                  preferred_element_type=F32)
            s_b = lax.dot_general(q_dec[CHUNK:], k_x, nt, preferred_element_type=F32)
            decay_col = jnp.transpose(
                jnp.broadcast_to(jnp.exp(b_last_a + b_last_b), (LANES, dk)))
            return s_a, s_b, q_int, k_st, decay_col

        def update_stage(h, s_a, s_b, q_int, k_st, decay_col):
            vs = slice(h * dv, (h + 1) * dv)
            v = v_ref[rs, vs]
            kv = lax.dot_general(k_st, v, tn, preferred_element_type=F32)
            state = state_ref[h]
            o_inter = _dot(q_int, state.astype(BF16))
            decay = jnp.concatenate([decay_col] * (dv // LANES), axis=1)
            state_ref[h] = state * decay + kv
            p_a = jnp.where(mask_a, s_a, 0.0).astype(BF16)
            p_b = jnp.where(mask_b, s_b, 0.0).astype(BF16)
            o = jnp.concatenate([_dot(p_a, v[:CHUNK]), _dot(p_b, v)], axis=0) + o_inter
            y = o * lax.rsqrt(jnp.mean(o * o, axis=-1, keepdims=True) + EPS) * nw
            merged_ref[rs, vs] = y.astype(BF16) * ga_ref[rs, vs] + yb_ref[rs, vs]

        staged = scores_stage(0)
        for h in range(GLA_HEADS):
            ahead = scores_stage(h + 1) if h + 1 < GLA_HEADS else None
            update_stage(h, *staged)
            staged = ahead

    def project(r, cs):
        o_ref[r, cs] = x_ref[r, cs] + _dot(merged_ref[r, :], wo_ref[:, cs])

    pairs_per_group = OUT_ROWS // PAIR
    out_cols = wo_ref.shape[1] // pairs_per_group
    pending = []
    for g in range(rows // OUT_ROWS):
        for c in range(g * pairs_per_group, (g + 1) * pairs_per_group):
            pair_body(c)
            if pending:
                project(*pending.pop(0))
        r = slice(g * OUT_ROWS, (g + 1) * OUT_ROWS)
        pending = [(r, slice(j * out_cols, (j + 1) * out_cols))
                   for j in range(pairs_per_group)]
    for piece in pending:
        project(*piece)


def _gla_out(qk, v, bcum, gate_a, y_b, norm_w, x, w_out, batch, seq, rows=512):
    m, vd = v.shape
    d = x.shape[1]
    assert w_out.shape == (vd, d)
    kd = bcum.shape[1]
    dk = kd // GLA_HEADS
    dv = vd // GLA_HEADS
    nblk = seq // rows
    assert qk.shape[1] == 2 * kd
    row_map = lambda b, s: (b * nblk + s, 0)
    kern = functools.partial(_gla_kernel, rows=rows, dk=dk, dv=dv)
    return pl.pallas_call(
        kern,
        out_shape=jax.ShapeDtypeStruct((m, d), F32),
        grid=(batch, nblk),
        in_specs=[pl.BlockSpec((rows, kd), lambda b, s: (b * nblk + s, 0)),
                  pl.BlockSpec((rows, kd), lambda b, s: (b * nblk + s, 1)),
                  pl.BlockSpec((rows, vd), row_map),
                  pl.BlockSpec((rows, kd), row_map),
                  pl.BlockSpec((rows, vd), row_map),
                  pl.BlockSpec((rows, vd), row_map),
                  pl.BlockSpec((1, dv), lambda b, s: (0, 0)),
                  pl.BlockSpec((rows, d), row_map),
                  pl.BlockSpec((vd, d), lambda b, s: (0, 0),
                               pipeline_mode=pl.Buffered(1))],
        out_specs=pl.BlockSpec((rows, d), row_map),
        scratch_shapes=[pltpu.VMEM((GLA_HEADS, dk, dv), F32),
                        pltpu.VMEM((rows, vd), BF16)],
        compiler_params=_params(("arbitrary", "arbitrary")),
        name="gla_out",
    )(qk, qk, v, bcum, gate_a, y_b, norm_w, x, w_out)


def _ffn_kernel(h_ref, nw_ref, wg_ref, wu_ref, wd_ref, fw_ref, o_ref, hn_ref):
    f = pl.program_id(1)

    @pl.when(f == 0)
    def _():
        h = h_ref[...]
        y = h * lax.rsqrt(jnp.mean(h * h, axis=-1, keepdims=True) + EPS)
        hn_ref[...] = (y * nw_ref[...]).astype(BF16)
        o_ref[...] = h

    hn = hn_ref[...]
    gate = _dot(hn, wg_ref[...])
    up = _dot(hn, wu_ref[...])
    act = (gate * _sigmoid(gate) * up).astype(BF16)
    o_ref[...] += _dot(act, wd_ref[...])

    @pl.when(f == pl.num_programs(1) - 1)
    def _():
        z = o_ref[...]
        y = z * lax.rsqrt(jnp.mean(z * z, axis=-1, keepdims=True) + EPS)
        o_ref[...] = y * fw_ref[...]


def _ffn(h1, norm_w, w_gate_up, wd, final_w, tm=1024, tf=512):
    m, d = h1.shape
    hidden = wd.shape[0]
    nf = hidden // tf
    row = pl.BlockSpec((tm, d), lambda i, f: (i, 0))
    vec = pl.BlockSpec((1, d), lambda i, f: (0, 0))
    return pl.pallas_call(
        _ffn_kernel,
        out_shape=jax.ShapeDtypeStruct((m, d), F32),
        grid=(m // tm, nf),
        in_specs=[row, vec,
                  pl.BlockSpec((d, tf), lambda i, f: (0, f)),
                  pl.BlockSpec((d, tf), lambda i, f: (0, nf + f)),
                  pl.BlockSpec((tf, d), lambda i, f: (f, 0)),
                  vec],
        out_specs=row,
        scratch_shapes=[pltpu.VMEM((tm, d), BF16)],
        compiler_params=_params(("arbitrary", "arbitrary")),
        name="ffn",
    )(h1, norm_w.reshape(1, d), w_gate_up, w_gate_up, wd, final_w.reshape(1, d))


def _mixer_layer(x2, batch, seq, mix_norm_w, w_in, w_gk_up, b_gk_up, gla_norm_w,
                 conv_w, w_out, ffn_norm_w, w_gate_up, w_down, final_w):
    d = x2.shape[1]
    kd = w_gk_up.shape[1]
    vd = d

    o_go = 2 * kd + vd
    o_gk = o_go + vd
    o_b = o_gk + GK_RANK
    o_c, o_x, o_ma, o_mb = o_b + d, o_b + 2 * d, o_b + 3 * d, o_b + 4 * d
    wt = jnp.swapaxes(w_in, 0, 1)
    w_gk = jnp.pad(wt[o_gk:o_b].astype(BF16), ((0, LANES - GK_RANK), (0, 0)))
    w_up = jnp.pad(w_gk_up.astype(BF16), ((0, LANES - GK_RANK), (0, 0)))
    conv_w8 = jnp.pad(conv_w, ((0, SUBLANES - CONV_K), (0, 0)))

    hn, bcum, qk = _norm_qk(x2, mix_norm_w, w_gk, w_up, b_gk_up.reshape(1, kd),
                            wt, 2 * kd)
    v = _in_plain(hn, wt, 2 * kd, vd, BF16)
    gate_a = _in_gate_a(hn, wt, o_go, o_ma, vd)
    y_b, w_out16, w_gate_up16, w_down16 = _in_conv(
        hn, wt, o_b, o_c, o_x, o_mb, d, conv_w8, seq, [w_out, w_gate_up, w_down])
    h1 = _gla_out(qk, v, bcum, gate_a, y_b, gla_norm_w.reshape(1, -1), x2, w_out16,
                  batch, seq)
    return _ffn(h1, ffn_norm_w, w_gate_up16, w_down16, final_w)


def kernel(x, mix_norm_w, w_in, w_gk_up, b_gk_up, gla_norm_w, conv_w, w_out,
           ffn_norm_w, w_gate_up, w_down, final_norm_w):
    batch, seq, d = x.shape
    assert mix_norm_w.shape[0] == 1, "single-layer block"
    out = _mixer_layer(x.reshape(batch * seq, d), batch, seq, mix_norm_w[0],
                       w_in[0], w_gk_up[0], b_gk_up[0], gla_norm_w[0], conv_w[0],
                       w_out[0], ffn_norm_w[0], w_gate_up[0], w_down[0],
                       final_norm_w)
    return out.reshape(batch, seq, d)
```

```python
import functools

import jax
import jax.numpy as jnp
from jax import lax
from jax.experimental import pallas as pl
from jax.experimental.pallas import tpu as pltpu

EPS = 1e-6
GLA_HEADS = 4
GK_RANK = 16
GATE_LOGIT_NORMALIZER = 16.0
CHUNK = 64
PAIR = 2 * CHUNK
CUM_ROWS = PAIR
OUT_ROWS = 2 * PAIR
CONV_K = 3
ROW_SUB = 256
ROW_SUB_LAST = 128

LANES = 128
SUBLANES = 8
VMEM_LIMIT = 56 * 1024 * 1024

F32 = jnp.float32
BF16 = jnp.bfloat16


def _params(semantics):
    return pltpu.CompilerParams(dimension_semantics=semantics,
                                vmem_limit_bytes=VMEM_LIMIT)


def _sigmoid(x):
    return 1.0 / (1.0 + jnp.exp(-x))


def _dot(a, b):
    return jnp.dot(a, b, preferred_element_type=F32)


def _norm_qk_kernel(x_ref, nw_ref, wgk_ref, wup_ref, b_ref, wt_ref,
                    hn_ref, bcum_ref, qk_ref, wbf_ref, *, tm):
    @pl.when(pl.program_id(0) == 0)
    def _():
        wbf_ref[...] = wt_ref[...].T.astype(BF16)

    row = lax.broadcasted_iota(jnp.int32, (CUM_ROWS, CUM_ROWS), 0)
    col = lax.broadcasted_iota(jnp.int32, (CUM_ROWS, CUM_ROWS), 1)
    tri = ((row >= col) & (row // CHUNK == col // CHUNK)).astype(BF16)
    subtiles = range(0, tm, CUM_ROWS)
    lows = []
    for r in subtiles:
        rows = slice(r, r + CUM_ROWS)
        x = x_ref[rows, :]
        y = x * lax.rsqrt(jnp.mean(x * x, axis=-1, keepdims=True) + EPS)
        hn = (y * nw_ref[...]).astype(BF16)
        hn_ref[rows, :] = hn
        lows.append(lax.dot_general(hn, wgk_ref[...], (((1,), (1,)), ((), ())),
                                    preferred_element_type=F32).astype(BF16))
    gks = [_dot(low, wup_ref[...]) + b_ref[...] for low in lows]
    for r, gk in zip(subtiles, gks):
        rows = slice(r, r + CUM_ROWS)
        if r % (4 * CUM_ROWS) == 0:
            wide = slice(r, r + 4 * CUM_ROWS)
            qk_ref[wide, :] = _dot(hn_ref[wide, :], wbf_ref[...]).astype(qk_ref.dtype)
        log_sig = jnp.minimum(gk, 0.0) - jnp.log(1.0 + jnp.exp(-jnp.abs(gk)))
        g = log_sig * (1.0 / GATE_LOGIT_NORMALIZER)
        g1 = g.astype(BF16)
        rem = g - g1.astype(F32)
        g2 = rem.astype(BF16)
        g3 = (rem - g2.astype(F32)).astype(BF16)
        bcum_ref[rows, :] = _dot(tri, g1) + _dot(tri, g2) + _dot(tri, g3)


def _norm_qk(x, norm_w, wgk, wup, b, wt, n_qk, tm=512):
    m, d = x.shape
    r = wgk.shape[0]
    n = wup.shape[1]
    row_block = lambda width: pl.BlockSpec((tm, width), lambda i: (i, 0))
    const_block = lambda shape: pl.BlockSpec(shape, lambda i: (0, 0))
    return pl.pallas_call(
        functools.partial(_norm_qk_kernel, tm=tm),
        out_shape=(jax.ShapeDtypeStruct((m, d), BF16),
                   jax.ShapeDtypeStruct((m, n), F32),
                   jax.ShapeDtypeStruct((m, n_qk), BF16)),
        grid=(m // tm,),
        in_specs=[row_block(d), const_block((1, d)), const_block((r, d)),
                  const_block((r, n)), const_block((1, n)),
                  pl.BlockSpec((pl.Element(n_qk), pl.Element(d)), lambda i: (0, 0),
                               pipeline_mode=pl.Buffered(1))],
        out_specs=(row_block(d), row_block(n), row_block(n_qk)),
        scratch_shapes=[pltpu.VMEM((d, n_qk), BF16)],
        compiler_params=_params(("arbitrary",)),
        name="norm_qk",
    )(x, norm_w.reshape(1, d), wgk, wup, b, wt)


def _weight_spec(d, tn, col0):
    assert col0 % SUBLANES == 0
    return pl.BlockSpec((pl.Element(tn), pl.Element(d)),
                        lambda j, i: (pl.multiple_of(col0 + j * tn, SUBLANES), 0))


def _load_weight(wt_ref):
    return wt_ref[...].T.astype(BF16)


def _row_subtiles(tm, sub=ROW_SUB):
    r = 0
    while tm - r > sub:
        yield r, sub
        r += sub
    while r < tm:
        yield r, min(sub, ROW_SUB_LAST)
        r += min(sub, ROW_SUB_LAST)


def _hn_spec(tm, d):
    return pl.BlockSpec((tm, d), lambda j, i: (i, 0))


def _out_spec(tm, tn):
    return pl.BlockSpec((tm, tn), lambda j, i: (i, j))


def _plain_kernel(h_ref, wt_ref, o_ref, wbf_ref, *, tm):
    @pl.when(pl.program_id(1) == 0)
    def _():
        wbf_ref[...] = _load_weight(wt_ref)

    for r in range(0, tm, 2 * ROW_SUB):
        rows = slice(r, r + 2 * ROW_SUB)
        o_ref[rows, :] = _dot(h_ref[rows, :], wbf_ref[...]).astype(o_ref.dtype)


def _in_plain(hn, wt, col0, n, out_dtype, tm=2048, tn=1024):
    m, d = hn.shape
    return pl.pallas_call(
        functools.partial(_plain_kernel, tm=tm),
        out_shape=jax.ShapeDtypeStruct((m, n), out_dtype),
        grid=(n // tn, m // tm),
        in_specs=[_hn_spec(tm, d), _weight_spec(d, tn, col0)],
        out_specs=_out_spec(tm, tn),
        scratch_shapes=[pltpu.VMEM((d, tn), BF16)],
        compiler_params=_params(("arbitrary", "arbitrary")),
        name="in_plain",
    )(hn, wt)


def _gate_a_kernel(h_ref, gt_ref, mt_ref, o_ref, wg_ref, wm_ref, *, tm):
    @pl.when(pl.program_id(1) == 0)
    def _():
        wg_ref[...] = _load_weight(gt_ref)
        wm_ref[...] = _load_weight(mt_ref)

    for r, n in _row_subtiles(tm, ROW_SUB_LAST):
        h = h_ref[r:r + n, :]
        g = _dot(h, wg_ref[...])
        m = _dot(h, wm_ref[...])
        o_ref[r:r + n, :] = (g * _sigmoid(g) * _sigmoid(m)).astype(o_ref.dtype)


def _in_gate_a(hn, wt, g0, m0, n, tm=2048, tn=512):
    m, d = hn.shape
    return pl.pallas_call(
        functools.partial(_gate_a_kernel, tm=tm),
        out_shape=jax.ShapeDtypeStruct((m, n), BF16),
        grid=(n // tn, m // tm),
        in_specs=[_hn_spec(tm, d), _weight_spec(d, tn, g0), _weight_spec(d, tn, m0)],
        out_specs=_out_spec(tm, tn),
        scratch_shapes=[pltpu.VMEM((d, tn), BF16)] * 2,
        compiler_params=_params(("arbitrary", "arbitrary")),
        name="in_gate_a",
    )(hn, wt, wt)


def _conv_kernel(*refs, tm, tiles_per_seq, n_cast):
    h_ref, bt_ref, ct_ref, xt_ref, mt_ref, cw_ref = refs[:6]
    cast_in = refs[6:6 + n_cast]
    o_ref = refs[6 + n_cast]
    cast_out = refs[7 + n_cast:7 + 2 * n_cast]
    wb_ref, wc_ref, wx_ref, wm_ref, work_ref, halo_ref = refs[7 + 2 * n_cast:]
    i = pl.program_id(1)

    @pl.when(i == 0)
    def _():
        wb_ref[...] = _load_weight(bt_ref)
        wc_ref[...] = _load_weight(ct_ref)
        wx_ref[...] = _load_weight(xt_ref)
        wm_ref[...] = _load_weight(mt_ref)

    @pl.when(i % tiles_per_seq == 0)
    def _():
        work_ref[0:SUBLANES, :] = jnp.zeros((SUBLANES, work_ref.shape[1]), F32)

    @pl.when(i % tiles_per_seq != 0)
    def _():
        work_ref[0:SUBLANES, :] = halo_ref[...]

    for src, dst in zip(cast_in, cast_out):
        dst[...] = src[...].astype(BF16)

    cw = cw_ref[...]
    for r, n in _row_subtiles(tm):
        h = h_ref[r:r + n, :]
        u = _dot(h, wc_ref[...]) * _dot(h, wx_ref[...])
        work_ref[SUBLANES + r:SUBLANES + r + n, :] = u
        u1 = work_ref[SUBLANES - 1 + r:SUBLANES - 1 + r + n, :]
        u2 = work_ref[SUBLANES - 2 + r:SUBLANES - 2 + r + n, :]
        conv = cw[0:1, :] * u2 + cw[1:2, :] * u1 + cw[2:3, :] * u
        y_b = _dot(h, wb_ref[...]) * conv
        o_ref[r:r + n, :] = (
            _sigmoid(_dot(h, wm_ref[...])) * y_b).astype(o_ref.dtype)
    halo_ref[...] = work_ref[tm:tm + SUBLANES, :]


def _cast_slab_spec(w, max_steps, n_row_tiles):
    steps = max(s for s in range(1, max_steps + 1)
                if w.shape[0] % s == 0 and (w.shape[0] // s) % (2 * SUBLANES) == 0)
    return pl.BlockSpec(
        (w.shape[0] // steps, w.shape[1]),
        lambda j, i: (jnp.minimum(j * n_row_tiles + i, steps - 1), 0))


def _in_conv(hn, wt, b0, c0, x0, m0, n, conv_w, seq, cast_weights, tm=2048, tn=256):
    m, d = hn.shape
    grid = (n // tn, m // tm)
    cast_specs = [_cast_slab_spec(w, grid[0] * grid[1], grid[1]) for w in cast_weights]
    kern = functools.partial(_conv_kernel, tm=tm, tiles_per_seq=seq // tm,
                             n_cast=len(cast_weights))
    return pl.pallas_call(
        kern,
        out_shape=[jax.ShapeDtypeStruct((m, n), BF16)]
                  + [jax.ShapeDtypeStruct(w.shape, BF16) for w in cast_weights],
        grid=grid,
        in_specs=[_hn_spec(tm, d)]
                 + [_weight_spec(d, tn, c) for c in (b0, c0, x0, m0)]
                 + [pl.BlockSpec((SUBLANES, tn), lambda j, i: (0, j))]
                 + cast_specs,
        out_specs=[_out_spec(tm, tn)] + cast_specs,
        scratch_shapes=[pltpu.VMEM((d, tn), BF16)] * 4
                       + [pltpu.VMEM((tm + SUBLANES, tn), F32),
                          pltpu.VMEM((SUBLANES, tn), F32)],
        compiler_params=_params(("arbitrary", "arbitrary")),
        name="in_conv",
    )(hn, wt, wt, wt, wt, conv_w, *cast_weights)


def _gla_kernel(q_ref, k_ref, v_ref, b_ref, ga_ref, yb_ref, nw_ref, x_ref, wo_ref,
                o_ref, state_ref, merged_ref, *, rows, dk, dv):
    @pl.when(pl.program_id(1) == 0)
    def _():
        state_ref[...] = jnp.zeros(state_ref.shape, F32)

    row = lax.broadcasted_iota(jnp.int32, (CHUNK, PAIR), 0)
    col = lax.broadcasted_iota(jnp.int32, (CHUNK, PAIR), 1)
    mask_a = (row >= col)[:, :CHUNK]
    mask_b = col <= row + CHUNK
    scale = dk ** -0.5
    nw = nw_ref[...]
    nt = (((1,), (1,)), ((), ()))
    tn = (((0,), (0,)), ((), ()))

    def pair_body(c):
        rs = slice(c * PAIR, (c + 1) * PAIR)

        def scores_stage(h):
            ks = slice(h * dk, (h + 1) * dk)
            b = b_ref[rs, ks]
            b_last_a = b[CHUNK - 1:CHUNK, :]
            b_last_b = b[PAIR - 1:PAIR, :]
            b_end = jnp.concatenate([jnp.broadcast_to(b_last_a, (CHUNK, dk)),
                                     jnp.broadcast_to(b_last_b, (CHUNK, dk))], axis=0)
            q = q_ref[rs, ks].astype(F32)
            k = k_ref[rs, ks].astype(F32)
            qd = q * (scale * jnp.exp(b))
            ke = k * jnp.exp(b_end - b)
            q_dec = qd.astype(BF16)
            k_dec = (k * jnp.exp(-b)).astype(BF16)
            k_end = ke.astype(BF16)
            q_int = jnp.concatenate(
                [q_dec[:CHUNK], (qd[CHUNK:] * jnp.exp(b_last_a)).astype(BF16)], axis=0)
            k_st = jnp.concatenate(
                [(ke[:CHUNK] * jnp.exp(b_last_b)).astype(BF16), k_end[CHUNK:]], axis=0)
            k_x = jnp.concatenate([k_end[:CHUNK], k_dec[CHUNK:]], axis=0)
            s_a = lax.dot_general(q_dec[:CHUNK], k_dec[:CHUNK], nt,
                                  preferred_element_type=F32)
            s_b = lax.dot_general(q_dec[CHUNK:], k_x, nt, preferred_element_type=F32)
            decay_col = jnp.transpose(
                jnp.broadcast_to(jnp.exp(b_last_a + b_last_b), (LANES, dk)))
            return s_a, s_b, q_int, k_st, decay_col

        def update_stage(h, s_a, s_b, q_int, k_st, decay_col):
            vs = slice(h * dv, (h + 1) * dv)
            v = v_ref[rs, vs]
            kv = lax.dot_general(k_st, v, tn, preferred_element_type=F32)
            state = state_ref[h]
            o_inter = _dot(q_int, state.astype(BF16))
            decay = jnp.concatenate([decay_col] * (dv // LANES), axis=1)
            state_ref[h] = state * decay + kv
            p_a = jnp.where(mask_a, s_a, 0.0).astype(BF16)
            p_b = jnp.where(mask_b, s_b, 0.0).astype(BF16)
            o = jnp.concatenate([_dot(p_a, v[:CHUNK]), _dot(p_b, v)], axis=0) + o_inter
            y = o * lax.rsqrt(jnp.mean(o * o, axis=-1, keepdims=True) + EPS) * nw
            merged_ref[rs, vs] = y.astype(BF16) * ga_ref[rs, vs] + yb_ref[rs, vs]

        staged = scores_stage(0)
        for h in range(GLA_HEADS):
            ahead = scores_stage(h + 1) if h + 1 < GLA_HEADS else None
            update_stage(h, *staged)
            staged = ahead

    def project(r, cs):
        o_ref[r, cs] = x_ref[r, cs] + _dot(merged_ref[r, :], wo_ref[:, cs])

    pairs_per_group = OUT_ROWS // PAIR
    out_cols = wo_ref.shape[1] // pairs_per_group
    pending = []
    for g in range(rows // OUT_ROWS):
        for c in range(g * pairs_per_group, (g + 1) * pairs_per_group):
            pair_body(c)
            if pending:
                project(*pending.pop(0))
        r = slice(g * OUT_ROWS, (g + 1) * OUT_ROWS)
        pending = [(r, slice(j * out_cols, (j + 1) * out_cols))
                   for j in range(pairs_per_group)]
    for piece in pending:
        project(*piece)


def _gla_out(qk, v, bcum, gate_a, y_b, norm_w, x, w_out, batch, seq, rows=512):
    m, vd = v.shape
    d = x.shape[1]
    assert w_out.shape == (vd, d)
    kd = bcum.shape[1]
    dk = kd // GLA_HEADS
    dv = vd // GLA_HEADS
    nblk = seq // rows
    assert qk.shape[1] == 2 * kd
    row_map = lambda b, s: (b * nblk + s, 0)
    kern = functools.partial(_gla_kernel, rows=rows, dk=dk, dv=dv)
    return pl.pallas_call(
        kern,
        out_shape=jax.ShapeDtypeStruct((m, d), F32),
        grid=(batch, nblk),
        in_specs=[pl.BlockSpec((rows, kd), lambda b, s: (b * nblk + s, 0)),
                  pl.BlockSpec((rows, kd), lambda b, s: (b * nblk + s, 1)),
                  pl.BlockSpec((rows, vd), row_map),
                  pl.BlockSpec((rows, kd), row_map),
                  pl.BlockSpec((rows, vd), row_map),
                  pl.BlockSpec((rows, vd), row_map),
                  pl.BlockSpec((1, dv), lambda b, s: (0, 0)),
                  pl.BlockSpec((rows, d), row_map),
                  pl.BlockSpec((vd, d), lambda b, s: (0, 0),
                               pipeline_mode=pl.Buffered(1))],
        out_specs=pl.BlockSpec((rows, d), row_map),
        scratch_shapes=[pltpu.VMEM((GLA_HEADS, dk, dv), F32),
                        pltpu.VMEM((rows, vd), BF16)],
        compiler_params=_params(("arbitrary", "arbitrary")),
        name="gla_out",
    )(qk, qk, v, bcum, gate_a, y_b, norm_w, x, w_out)


def _ffn_kernel(h_ref, nw_ref, wg_ref, wu_ref, wd_ref, fw_ref, o_ref, hn_ref):
    f = pl.program_id(1)

    @pl.when(f == 0)
    def _():
        h = h_ref[...]
        y = h * lax.rsqrt(jnp.mean(h * h, axis=-1, keepdims=True) + EPS)
        hn_ref[...] = (y * nw_ref[...]).astype(BF16)
        o_ref[...] = h

    hn = hn_ref[...]
    gate = _dot(hn, wg_ref[...])
    up = _dot(hn, wu_ref[...])
    act = (gate * _sigmoid(gate) * up).astype(BF16)
    o_ref[...] += _dot(act, wd_ref[...])

    @pl.when(f == pl.num_programs(1) - 1)
    def _():
        z = o_ref[...]
        y = z * lax.rsqrt(jnp.mean(z * z, axis=-1, keepdims=True) + EPS)
        o_ref[...] = y * fw_ref[...]


def _ffn(h1, norm_w, w_gate_up, wd, final_w, tm=1024, tf=512):
    m, d = h1.shape
    hidden = wd.shape[0]
    nf = hidden // tf
    row = pl.BlockSpec((tm, d), lambda i, f: (i, 0))
    vec = pl.BlockSpec((1, d), lambda i, f: (0, 0))
    return pl.pallas_call(
        _ffn_kernel,
        out_shape=jax.ShapeDtypeStruct((m, d), F32),
        grid=(m // tm, nf),
        in_specs=[row, vec,
                  pl.BlockSpec((d, tf), lambda i, f: (0, f)),
                  pl.BlockSpec((d, tf), lambda i, f: (0, nf + f)),
                  pl.BlockSpec((tf, d), lambda i, f: (f, 0)),
                  vec],
        out_specs=row,
        scratch_shapes=[pltpu.VMEM((tm, d), BF16)],
        compiler_params=_params(("arbitrary", "arbitrary")),
        name="ffn",
    )(h1, norm_w.reshape(1, d), w_gate_up, w_gate_up, wd, final_w.reshape(1, d))


def _mixer_layer(x2, batch, seq, mix_norm_w, w_in, w_gk_up, b_gk_up, gla_norm_w,
                 conv_w, w_out, ffn_norm_w, w_gate_up, w_down, final_w):
    d = x2.shape[1]
    kd = w_gk_up.shape[1]
    vd = d

    o_go = 2 * kd + vd
    o_gk = o_go + vd
    o_b = o_gk + GK_RANK
    o_c, o_x, o_ma, o_mb = o_b + d, o_b + 2 * d, o_b + 3 * d, o_b + 4 * d
    wt = jnp.swapaxes(w_in, 0, 1)
    w_gk = jnp.pad(wt[o_gk:o_b].astype(BF16), ((0, LANES - GK_RANK), (0, 0)))
    w_up = jnp.pad(w_gk_up.astype(BF16), ((0, LANES - GK_RANK), (0, 0)))
    conv_w8 = jnp.pad(conv_w, ((0, SUBLANES - CONV_K), (0, 0)))

    hn, bcum, qk = _norm_qk(x2, mix_norm_w, w_gk, w_up, b_gk_up.reshape(1, kd),
                            wt, 2 * kd)
    v = _in_plain(hn, wt, 2 * kd, vd, BF16)
    gate_a = _in_gate_a(hn, wt, o_go, o_ma, vd)
    y_b, w_out16, w_gate_up16, w_down16 = _in_conv(
        hn, wt, o_b, o_c, o_x, o_mb, d, conv_w8, seq, [w_out, w_gate_up, w_down])
    h1 = _gla_out(qk, v, bcum, gate_a, y_b, gla_norm_w.reshape(1, -1), x2, w_out16,
                  batch, seq)
    return _ffn(h1, ffn_norm_w, w_gate_up16, w_down16, final_w)


def kernel(x, mix_norm_w, w_in, w_gk_up, b_gk_up, gla_norm_w, conv_w, w_out,
           ffn_norm_w, w_gate_up, w_down, final_norm_w):
    batch, seq, d = x.shape
    assert mix_norm_w.shape[0] == 1, "single-layer block"
    out = _mixer_layer(x.reshape(batch * seq, d), batch, seq, mix_norm_w[0],
                       w_in[0], w_gk_up[0], b_gk_up[0], gla_norm_w[0], conv_w[0],
                       w_out[0], ffn_norm_w[0], w_gate_up[0], w_down[0],
                       final_norm_w)
    return out.reshape(batch, seq, d)
```
